```python
import math
import jax, jax.numpy as jnp
from jax import lax
import numpy as np

D_MODEL = 2048
BATCH = 16
SEQ = 2048
DEPTH = 4

EPS = 1e-5
SSD_HEAD_DIM = 64
D_SSD = D_MODEL
H_SSD = D_SSD // SSD_HEAD_DIM
SSD_GROUPS = 8
SSD_STATE = 128
CONV_K = 4
CONV_DIM = D_SSD + 2 * SSD_GROUPS * SSD_STATE
SSD_CHUNK = 128
DT_MIN = 1e-3
DT_MAX = 1e-1
FOX_HEAD_DIM = 128
D_FOX = D_MODEL
H_FOX = D_FOX // FOX_HEAD_DIM
Q_BLOCK = 128
FORGET_BIAS_INIT = 3.0
N_EXPERTS = 32
TOP_K = 4
D_FF_EXPERT = 3 * D_MODEL // 8
SWIGLU_LIMIT = 7.0
SWIGLU_ALPHA = 1.702
EXPERT_BLOCK = 256
IN_SIZES = (D_SSD, CONV_DIM, H_SSD, 3 * D_FOX, H_FOX, 2 * D_MODEL)
IN_COLS = sum(IN_SIZES)
IN_SPLITS = tuple(int(s) for s in np.cumsum(IN_SIZES)[:-1])

kernel_name = 'hybrid_ssd_fox_moe_trunk'


def rmsnorm(x, w):
    xf = x.astype(jnp.float32)
    y = xf * lax.rsqrt(jnp.mean(xf * xf, axis=-1, keepdims=True) + EPS)
    return (y * w.astype(jnp.float32)).astype(x.dtype)


def causal_depthwise_conv(u, w, b):
    k, c = w.shape
    out = lax.conv_general_dilated(
        u, w[:, None, :].astype(u.dtype), window_strides=(1,), padding=[(k - 1, 0)],
        dimension_numbers=('NWC', 'WIO', 'NWC'), feature_group_count=c)
    return out + b.astype(u.dtype)


def ssd_chunked_scan(x, dt, a, bm, cm):
    b, t, h, p = x.shape
    g, n = bm.shape[2], bm.shape[3]
    r = h // g
    nc = t // SSD_CHUNK
    xd = (x * dt[..., None]).reshape(b, nc, SSD_CHUNK, g, r, p)
    da = (dt * a).reshape(b, nc, SSD_CHUNK, g, r).transpose(0, 3, 4, 1, 2)
    bc = bm.reshape(b, nc, SSD_CHUNK, g, n)
    cc = cm.reshape(b, nc, SSD_CHUNK, g, n)
    a_cs = jnp.cumsum(da, axis=-1)
    causal = jnp.tril(jnp.ones((SSD_CHUNK, SSD_CHUNK), dtype=bool))
    seg = a_cs[..., :, None] - a_cs[..., None, :]
    lmat = jnp.where(causal, jnp.exp(jnp.where(causal, seg, 0.0)), 0.0)
    cb = jnp.einsum('bclgn,bcsgn->bgcls', cc, bc)
    y_diag = jnp.einsum('bgcls,bgrcls,bcsgrp->bclgrp', cb, lmat, xd)
    decay_states = jnp.exp(a_cs[..., -1:] - a_cs)
    states = jnp.einsum('bclgn,bgrcl,bclgrp->cbgrpn', bc, decay_states, xd)
    chunk_decay = jnp.exp(a_cs[..., -1]).transpose(3, 0, 1, 2)

    def step(state, inp):
        s_c, d_c = inp
        return d_c[..., None, None] * state + s_c, state

    _, prev = lax.scan(step, jnp.zeros(states.shape[1:], jnp.float32), (states, chunk_decay))
    y_off = jnp.einsum('bclgn,cbgrpn,bgrcl->bclgrp', cc, prev, jnp.exp(a_cs))
    return (y_diag + y_off).reshape(b, t, h, p)


def gated_group_rmsnorm(y, z, w):
    gz = y * jax.nn.silu(z.astype(jnp.float32))
    gz = gz.reshape(*gz.shape[:-1], SSD_GROUPS, D_SSD // SSD_GROUPS)
    gz = gz * lax.rsqrt(jnp.mean(gz * gz, axis=-1, keepdims=True) + EPS)
    return (gz.reshape(y.shape) * w.astype(jnp.float32)).astype(z.dtype)


def forgetting_attention(q, k, v, log_f):
    t = q.shape[1]
    scale = 1.0 / math.sqrt(q.shape[-1])
    c = jnp.cumsum(log_f, axis=1).transpose(0, 2, 1)
    outs = []
    for i in range(t // Q_BLOCK):
        s0 = i * Q_BLOCK
        e = s0 + Q_BLOCK
        s = jnp.einsum('bqhd,bkhd->bhqk', q[:, s0:e], k[:, :e],
                       preferred_element_type=jnp.float32) * scale
        s = s + c[:, :, s0:e, None] - c[:, :, None, :e]
        mask = (s0 + jnp.arange(Q_BLOCK))[:, None] >= jnp.arange(e)[None, :]
        p = jax.nn.softmax(jnp.where(mask, s, -jnp.inf), axis=-1)
        outs.append(jnp.einsum('bhqk,bkhd->bqhd', p.astype(v.dtype), v[:, :e]))
    return jnp.concatenate(outs, axis=1)


def hybrid_mixer(h, w_in, conv_w, conv_b, dt_bias, a_log, d_skip, ssd_norm_w, b_forget,
                 w_branch_ssd, w_branch_fox, w_out):
    b, t, _ = h.shape
    proj = jnp.einsum('btd,dc->btc', h, w_in)
    z, xbc, dt_raw, qkv, f_raw, g_raw = jnp.split(proj, IN_SPLITS, axis=-1)
    xbc = jax.nn.silu(causal_depthwise_conv(xbc, conv_w, conv_b))
    xs, bm, cm = jnp.split(xbc, [D_SSD, D_SSD + SSD_GROUPS * SSD_STATE], axis=-1)
    dt = jax.nn.softplus(dt_raw.astype(jnp.float32) + dt_bias.astype(jnp.float32))
    a = -jnp.exp(a_log.astype(jnp.float32))
    xs = xs.astype(jnp.float32).reshape(b, t, H_SSD, SSD_HEAD_DIM)
    y = ssd_chunked_scan(xs, dt, a,
                         bm.astype(jnp.float32).reshape(b, t, SSD_GROUPS, SSD_STATE),
                         cm.astype(jnp.float32).reshape(b, t, SSD_GROUPS, SSD_STATE))
    y = y + d_skip.astype(jnp.float32)[:, None] * xs
    y_ssd = gated_group_rmsnorm(y.reshape(b, t, D_SSD), z, ssd_norm_w)
    qkv = qkv.reshape(b, t, 3, H_FOX, FOX_HEAD_DIM)
    log_f = jax.nn.log_sigmoid(f_raw.astype(jnp.float32) + b_forget.astype(jnp.float32))
    o_fox = forgetting_attention(qkv[:, :, 0], qkv[:, :, 1], qkv[:, :, 2], log_f)
    o_fox = o_fox.reshape(b, t, D_FOX)
    g_ssd, g_fox = jnp.split(g_raw, 2, axis=-1)
    u = (jax.nn.sigmoid(g_ssd) * jnp.einsum('btc,cd->btd', y_ssd, w_branch_ssd)
         + jax.nn.sigmoid(g_fox) * jnp.einsum('btc,cd->btd', o_fox, w_branch_fox))
    return jnp.einsum('btd,de->bte', u, w_out)


def moe_ffn(h, router_w, router_b, w_gate, b_gate, w_up, b_up, w_down, b_down):
    n_tok = h.shape[0]
    logits = (h @ router_w).astype(jnp.float32) + router_b.astype(jnp.float32)
    top_logit, top_idx = lax.top_k(logits, TOP_K)
    gates = jax.nn.softmax(top_logit, axis=-1)
    n_pairs = n_tok * TOP_K
    n_blocks = (n_pairs + N_EXPERTS * (EXPERT_BLOCK - 1) + EXPERT_BLOCK - 1) // EXPERT_BLOCK
    n_slots = n_blocks * EXPERT_BLOCK
    expert_flat = top_idx.reshape(-1).astype(jnp.int32)
    order = jnp.argsort(expert_flat)
    sorted_expert = expert_flat[order]
    counts = jnp.bincount(expert_flat, length=N_EXPERTS)
    padded = (counts + EXPERT_BLOCK - 1) // EXPERT_BLOCK * EXPERT_BLOCK
    padded_end = jnp.cumsum(padded)
    start = jnp.cumsum(counts) - counts
    rank = jnp.arange(n_pairs, dtype=jnp.int32) - start[sorted_expert]
    dest = (padded_end - padded)[sorted_expert] + rank
    slot_token = jnp.zeros((n_slots,), jnp.int32).at[dest].set((order // TOP_K).astype(jnp.int32))
    slot_gate = jnp.zeros((n_slots,), jnp.float32).at[dest].set(gates.reshape(-1)[order])
    block_start = jnp.arange(n_blocks, dtype=jnp.int32) * EXPERT_BLOCK
    block_expert = jnp.minimum(jnp.searchsorted(padded_end, block_start, side='right'),
                               N_EXPERTS - 1).astype(jnp.int32)

    def expert_block(acc, blk):
        tok, g, e = blk
        xb = h[tok]
        a = jnp.minimum(xb @ w_gate[e] + b_gate[e], SWIGLU_LIMIT)
        up = jnp.clip(xb @ w_up[e] + b_up[e], -SWIGLU_LIMIT, SWIGLU_LIMIT)
        y = ((up + 1.0) * (a * jax.nn.sigmoid(SWIGLU_ALPHA * a))) @ w_down[e] + b_down[e]
        return acc.at[tok].add(g[:, None] * y.astype(jnp.float32)), None

    acc, _ = lax.scan(expert_block, jnp.zeros(h.shape, jnp.float32),
                      (slot_token.reshape(n_blocks, EXPERT_BLOCK),
                       slot_gate.reshape(n_blocks, EXPERT_BLOCK), block_expert))
    return acc.astype(h.dtype)


def setup_inputs(seed: int = 0) -> dict:
    key = jax.random.key(seed)
    ks = jax.random.split(key, 24)
    f32 = jnp.float32
    L = DEPTH

    def nrm(k, shape, scale):
        return jax.random.normal(k, shape, f32) * scale

    dt0 = jnp.exp(jax.random.uniform(ks[5], (L, H_SSD), f32, math.log(DT_MIN), math.log(DT_MAX)))
    out_scale = (2 * DEPTH) ** -0.5
    return {
        'x': nrm(ks[0], (BATCH, SEQ, D_MODEL), 1.0),
        'norm_mix_w': 1.0 + nrm(ks[1], (L, D_MODEL), 0.02),
        'w_in': nrm(ks[2], (L, D_MODEL, IN_COLS), D_MODEL ** -0.5),
        'conv_w': nrm(ks[3], (L, CONV_K, CONV_DIM), CONV_K ** -0.5),
        'conv_b': nrm(ks[4], (L, CONV_DIM), 0.02),
        'dt_bias': dt0 + jnp.log(-jnp.expm1(-dt0)),
        'a_log': jnp.log(jax.random.uniform(ks[6], (L, H_SSD), f32, 1.0, 16.0)),
        'd_skip': 1.0 + nrm(ks[7], (L, H_SSD), 0.1),
        'ssd_norm_w': 1.0 + nrm(ks[8], (L, D_SSD), 0.02),
        'b_forget': FORGET_BIAS_INIT + nrm(ks[9], (L, H_FOX), 0.5),
        'w_branch_ssd': nrm(ks[10], (L, D_SSD, D_MODEL), D_SSD ** -0.5),
        'w_branch_fox': nrm(ks[11], (L, D_FOX, D_MODEL), D_FOX ** -0.5),
        'w_out': nrm(ks[12], (L, D_MODEL, D_MODEL), D_MODEL ** -0.5 * out_scale),
        'norm_ffn_w': 1.0 + nrm(ks[13], (L, D_MODEL), 0.02),
        'router_w': nrm(ks[14], (L, D_MODEL, N_EXPERTS), D_MODEL ** -0.5),
        'router_b': nrm(ks[15], (L, N_EXPERTS), 0.01),
        'w_gate': nrm(ks[16], (L, N_EXPERTS, D_MODEL, D_FF_EXPERT), D_MODEL ** -0.5),
        'b_gate': nrm(ks[17], (L, N_EXPERTS, D_FF_EXPERT), 0.02),
        'w_up': nrm(ks[18], (L, N_EXPERTS, D_MODEL, D_FF_EXPERT), D_MODEL ** -0.5),
        'b_up': nrm(ks[19], (L, N_EXPERTS, D_FF_EXPERT), 0.02),
        'w_down': nrm(ks[20], (L, N_EXPERTS, D_FF_EXPERT, D_MODEL), D_FF_EXPERT ** -0.5 * out_scale),
        'b_down': nrm(ks[21], (L, N_EXPERTS, D_MODEL), 0.02),
        'norm_final_w': 1.0 + nrm(ks[22], (D_MODEL,), 0.02),
    }


def reference(x, norm_mix_w, w_in, conv_w, conv_b, dt_bias, a_log, d_skip, ssd_norm_w, b_forget,
              w_branch_ssd, w_branch_fox, w_out, norm_ffn_w, router_w, router_b,
              w_gate, b_gate, w_up, b_up, w_down, b_down, norm_final_w):
    b, t, d = x.shape
    for l in range(DEPTH):
        h = rmsnorm(x, norm_mix_w[l])
        x = x + hybrid_mixer(h, w_in[l], conv_w[l], conv_b[l], dt_bias[l], a_log[l], d_skip[l],
                             ssd_norm_w[l], b_forget[l], w_branch_ssd[l], w_branch_fox[l], w_out[l])
        h = rmsnorm(x, norm_ffn_w[l]).reshape(b * t, d)
        x = x + moe_ffn(h, router_w[l], router_b[l], w_gate[l], b_gate[l], w_up[l], b_up[l],
                        w_down[l], b_down[l]).reshape(b, t, d)
    return rmsnorm(x, norm_final_w)
```

```python
import functools
import math

import jax
import jax.numpy as jnp
from jax import lax
from jax.experimental import pallas as pl
from jax.experimental.pallas import tpu as pltpu

F32 = jnp.float32
BF16 = jnp.bfloat16

D_MODEL = 2048
EPS = 1e-5
SSD_HEAD_DIM = 64
H_SSD = 32
SSD_GROUPS = 8
SSD_STATE = 128
CONV_K = 4
CONV_DIM = D_MODEL + 2 * SSD_GROUPS * SSD_STATE
SSD_CHUNK = 128
FOX_HEAD_DIM = 128
H_FOX = 16
N_EXPERTS = 32
TOP_K = 4
D_FF = 768
SWIGLU_LIMIT = 7.0
SWIGLU_ALPHA = 1.702

LANES = 128
VMEM_LIMIT = 56 * 1024 * 1024
NEG_BIG = -1e30

COL_XBC, COL_Z, COL_GSSD, COL_Q, COL_K, COL_V, COL_GFOX = 0, 2, 3, 4, 5, 6, 7
N_MAIN = 8 * D_MODEL
SM_F0, SM_DT0 = 0, H_FOX

MOE_BLOCK = 512


def _cparams(sem):
    return pltpu.CompilerParams(dimension_semantics=sem, vmem_limit_bytes=VMEM_LIMIT)


def _dot(a, b):
    return jnp.dot(a, b, preferred_element_type=F32)


def _dot_nt(a, b):
    return lax.dot_general(a, b, (((1,), (1,)), ((), ())), preferred_element_type=F32)


def _sigmoid(x):
    return 1.0 / (1.0 + jnp.exp(-x))


def _rmsnorm_kernel(x_ref, w_ref, o_ref):
    x = x_ref[...]
    y = x * lax.rsqrt(jnp.mean(x * x, axis=-1, keepdims=True) + EPS) * w_ref[...]
    o_ref[...] = y.astype(o_ref.dtype)


def rmsnorm(x, w, out_dtype, tm=512):
    n, d = x.shape
    tm = min(tm, n)
    return pl.pallas_call(
        _rmsnorm_kernel,
        out_shape=jax.ShapeDtypeStruct((n, d), out_dtype),
        grid=(n // tm,),
        in_specs=[pl.BlockSpec((tm, d), lambda i: (i, 0)), pl.BlockSpec((1, d), lambda i: (0, 0))],
        out_specs=pl.BlockSpec((tm, d), lambda i: (i, 0)),
        compiler_params=_cparams(("parallel",)),
    )(x, w.reshape(1, d))


def _mm_kernel(a_ref, w_ref, o_ref):
    o_ref[...] = _dot(a_ref[...], w_ref[...]).astype(o_ref.dtype)


def matmul(a, w, out_dtype, tm=1024, tn=1024):
    m, k = a.shape
    _, n = w.shape
    tm, tn = min(tm, m), min(tn, n)
    return pl.pallas_call(
        _mm_kernel,
        out_shape=jax.ShapeDtypeStruct((m, n), out_dtype),
        grid=(m // tm, n // tn),
        in_specs=[pl.BlockSpec((tm, k), lambda i, j: (i, 0)), pl.BlockSpec((k, tn), lambda i, j: (0, j))],
        out_specs=pl.BlockSpec((tm, tn), lambda i, j: (i, j)),
        compiler_params=_cparams(("parallel", "arbitrary")),
    )(a, w)


def _cumsum_rows(val):
    L = val.shape[0]
    row = lax.broadcasted_iota(jnp.int32, (L, L), 0)
    col = lax.broadcasted_iota(jnp.int32, (L, L), 1)
    tri = jnp.where(row >= col, 1.0, 0.0).astype(BF16)
    hi = val.astype(BF16)
    r1 = val - hi.astype(F32)
    mid = r1.astype(BF16)
    lo = (r1 - mid.astype(F32)).astype(BF16)
    return _dot(tri, hi) + _dot(tri, mid) + _dot(tri, lo)


def _ssd_kernel(xbc_ref, z_ref, sm_ref, convw_ref, convb_ref, bias_ref, alog_ref, dskip_ref, normw_ref,
                y_ref, c_ref, ct_ref, ubuf, xc, ybuf, state, carry):
    L = SSD_CHUNK
    ci = pl.program_id(1)

    @pl.when(ci == 0)
    def _():
        ubuf[0:8, :] = jnp.zeros((8, CONV_DIM), F32)
        state[...] = jnp.zeros(state.shape, F32)
        carry[...] = jnp.zeros(carry.shape, F32)

    ubuf[8:8 + L, :] = xbc_ref[...].astype(F32)
    ct_w = 512
    for t in range(CONV_DIM // ct_w):
        sl = slice(t * ct_w, (t + 1) * ct_w)
        acc = jnp.broadcast_to(convb_ref[:, sl], (L, ct_w))
        for j in range(CONV_K):
            acc = acc + convw_ref[j:j + 1, sl] * ubuf[8 - (CONV_K - 1) + j:8 - (CONV_K - 1) + j + L, sl]
        xc[:, sl] = acc * _sigmoid(acc)
    ubuf[0:8, :] = ubuf[L:L + 8, :]

    lane = lax.broadcasted_iota(jnp.int32, (L, LANES), 1)
    lane1 = lax.broadcasted_iota(jnp.int32, (1, LANES), 1)
    v = sm_ref[...] + bias_ref[...]
    sp = jnp.log1p(jnp.exp(-jnp.abs(v)))
    logf = jnp.minimum(v, 0.0) - sp
    dt = jnp.maximum(v, 0.0) + sp
    is_dt1 = (lane1 >= SM_DT0) & (lane1 < SM_DT0 + H_SSD)
    a_row = jnp.where(is_dt1, -jnp.exp(alog_ref[...]), 0.0)
    cs = _cumsum_rows(jnp.where(lane < SM_DT0, logf, dt * a_row))
    c_full = cs + carry[...]
    c_ref[...] = c_full
    ct_ref[0] = c_full.T[0:H_FOX, :]
    carry[...] = c_full[L - 1:L, :]

    cs_t = cs.T
    last = cs[L - 1:L, :]
    e_in = jnp.exp(cs)
    e_out = jnp.exp(last - cs)
    e_all = jnp.exp(last)
    row = lax.broadcasted_iota(jnp.int32, (L, L), 0)
    col = lax.broadcasted_iota(jnp.int32, (L, L), 1)
    causal = row >= col
    lo_half = lane < SSD_HEAD_DIM
    lo_half1 = lane1 < SSD_HEAD_DIM

    for g in range(SSD_GROUPS):
        bm = xc[:, D_MODEL + g * SSD_STATE:D_MODEL + (g + 1) * SSD_STATE]
        cm = xc[:, D_MODEL + (SSD_GROUPS + g) * SSD_STATE:D_MODEL + (SSD_GROUPS + g + 1) * SSD_STATE]
        bmb = bm.astype(BF16)
        cmb = cm.astype(BF16)
        cb = _dot_nt(cmb, bmb)
        bm_t = bm.T.astype(BF16)
        for pr in range(2):
            h0 = g * 4 + pr * 2
            k0, k1 = SM_DT0 + h0, SM_DT0 + h0 + 1
            pidx = h0 // 2

            def pair(qarr, k0=k0, k1=k1):
                return jnp.where(lo_half, qarr[:, k0:k0 + 1], qarr[:, k1:k1 + 1])

            xs_p = xc[:, h0 * SSD_HEAD_DIM:h0 * SSD_HEAD_DIM + LANES]
            xd = xs_p * pair(dt)
            xdb = xd.astype(BF16)
            ys = []
            for k in (k0, k1):
                seg = cs[:, k:k + 1] - cs_t[k:k + 1, :]
                lm = jnp.exp(jnp.where(causal, seg, NEG_BIG))
                ys.append(_dot((cb * lm).astype(BF16), xdb))
            y_diag = jnp.where(lo_half, ys[0], ys[1])
            st_prev = state[pidx]
            y_off = _dot(cmb, st_prev.astype(BF16)) * pair(e_in)
            st_new = _dot(bm_t, (xd * pair(e_out)).astype(BF16))
            e_pair = jnp.where(lo_half1, e_all[:, k0:k0 + 1], e_all[:, k1:k1 + 1])
            state[pidx] = e_pair * st_prev + st_new
            psl = slice(h0 * SSD_HEAD_DIM, h0 * SSD_HEAD_DIM + LANES)
            ybuf[:, psl] = y_diag + y_off + dskip_ref[:, psl] * xs_p

    gw = D_MODEL // SSD_GROUPS
    for g in range(SSD_GROUPS):
        sl = slice(g * gw, (g + 1) * gw)
        zz = z_ref[:, sl].astype(F32)
        gz = ybuf[:, sl] * (zz * _sigmoid(zz))
        gz = gz * lax.rsqrt(jnp.mean(gz * gz, axis=-1, keepdims=True) + EPS)
        y_ref[:, sl] = (gz * normw_ref[:, sl]).astype(y_ref.dtype)


def ssd_branch(pmain, psmall, conv_w, conv_b, sm_bias, alog_row, dskip_row, norm_w, batch, seq):
    n = batch * seq
    L = SSD_CHUNK
    nc = seq // L
    rowmap = lambda b, c: (b * nc + c, 0)
    const = lambda b, c: (0, 0)
    return pl.pallas_call(
        _ssd_kernel,
        out_shape=(jax.ShapeDtypeStruct((n, D_MODEL), BF16),
                   jax.ShapeDtypeStruct((n, LANES), F32),
                   jax.ShapeDtypeStruct((batch, H_FOX, seq), F32)),
        grid=(batch, nc),
        in_specs=[
            pl.BlockSpec((L, CONV_DIM), lambda b, c: (b * nc + c, COL_XBC)),
            pl.BlockSpec((L, D_MODEL), lambda b, c: (b * nc + c, COL_Z)),
            pl.BlockSpec((L, LANES), rowmap),
            pl.BlockSpec((CONV_K, CONV_DIM), const),
            pl.BlockSpec((1, CONV_DIM), const),
            pl.BlockSpec((1, LANES), const),
            pl.BlockSpec((1, LANES), const),
            pl.BlockSpec((1, D_MODEL), const),
            pl.BlockSpec((1, D_MODEL), const),
        ],
        out_specs=(pl.BlockSpec((L, D_MODEL), rowmap),
                   pl.BlockSpec((L, LANES), rowmap),
                   pl.BlockSpec((1, H_FOX, L), lambda b, c: (b, 0, c))),
        scratch_shapes=[
            pltpu.VMEM((L + 8, CONV_DIM), F32),
            pltpu.VMEM((L, CONV_DIM), F32),
            pltpu.VMEM((L, D_MODEL), F32),
            pltpu.VMEM((H_SSD // 2, SSD_STATE, LANES), F32),
            pltpu.VMEM((1, LANES), F32),
        ],
        compiler_params=_cparams(("parallel", "arbitrary")),
    )(pmain, pmain, psmall, conv_w, conv_b, sm_bias, alog_row, dskip_row, norm_w)


def _fox_kernel(q_ref, k_ref, v_ref, c_ref, ct_ref, o_ref, *, tq, scale):
    h = pl.program_id(1)
    qi = pl.program_id(2)
    q = q_ref[...]
    lane = lax.broadcasted_iota(jnp.int32, (tq, LANES), 1)
    cq = jnp.sum(jnp.where(lane == h, c_ref[...], 0.0), axis=1, keepdims=True)
    row = lax.broadcasted_iota(jnp.int32, (tq, tq), 0)
    col = lax.broadcasted_iota(jnp.int32, (tq, tq), 1)

    def step(j, carry, masked):
        m, l, acc = carry
        off = pl.multiple_of(j * tq, tq)
        ks = k_ref[pl.ds(off, tq), :]
        vs = v_ref[pl.ds(off, tq), :]
        s = _dot_nt(q, ks) * scale + (cq - ct_ref[0, j])
        if masked:
            s = jnp.where(row >= col, s, NEG_BIG)
        m_new = jnp.maximum(m, jnp.max(s, axis=1, keepdims=True))
        p = jnp.exp(s - m_new)
        alpha = jnp.exp(m - m_new)
        l = alpha * l + jnp.sum(p, axis=1, keepdims=True)
        acc = alpha * acc + _dot(p.astype(BF16), vs)
        return m_new, l, acc

    init = (jnp.full((tq, 1), NEG_BIG, F32), jnp.zeros((tq, 1), F32), jnp.zeros((tq, FOX_HEAD_DIM), F32))
    carry = lax.fori_loop(0, qi, lambda j, c: step(j, c, False), init)
    _, l, acc = step(qi, carry, True)
    o_ref[...] = (acc / l).astype(o_ref.dtype)


def fox_attention(pmain, c, ct, batch, seq, tq=256):
    n = batch * seq
    tq = min(tq, seq)
    nq = seq // tq
    ct4 = ct.reshape(batch * H_FOX, nq, 1, tq)
    per_d = D_MODEL // FOX_HEAD_DIM
    return pl.pallas_call(
        functools.partial(_fox_kernel, tq=tq, scale=1.0 / math.sqrt(FOX_HEAD_DIM)),
        out_shape=jax.ShapeDtypeStruct((n, D_MODEL), BF16),
        grid=(batch, H_FOX, nq),
        in_specs=[
            pl.BlockSpec((tq, FOX_HEAD_DIM), lambda b, h, i: (b * nq + i, COL_Q * per_d + h)),
            pl.BlockSpec((seq, FOX_HEAD_DIM), lambda b, h, i: (b, COL_K * per_d + h)),
            pl.BlockSpec((seq, FOX_HEAD_DIM), lambda b, h, i: (b, COL_V * per_d + h)),
            pl.BlockSpec((tq, LANES), lambda b, h, i: (b * nq + i, 0)),
            pl.BlockSpec((1, nq, 1, tq), lambda b, h, i: (b * H_FOX + h, 0, 0, 0)),
        ],
        out_specs=pl.BlockSpec((tq, FOX_HEAD_DIM), lambda b, h, i: (b * nq + i, h)),
        compiler_params=_cparams(("parallel", "parallel", "arbitrary")),
    )(pmain, pmain, pmain, c, ct4)


def _merge_kernel(y_ref, o_ref, wbs_ref, wbf_ref, gs_ref, gf_ref, u_ref):
    a = _dot(y_ref[...], wbs_ref[...])
    b = _dot(o_ref[...], wbf_ref[...])
    u = _sigmoid(gs_ref[...].astype(F32)) * a + _sigmoid(gf_ref[...].astype(F32)) * b
    u_ref[...] = u.astype(u_ref.dtype)


def branch_merge(y_ssd, o_fox, wbs, wbf, pmain, tm=512, tn=1024):
    n = y_ssd.shape[0]
    tm = min(tm, n)
    per_d = D_MODEL // tn
    return pl.pallas_call(
        _merge_kernel,
        out_shape=jax.ShapeDtypeStruct((n, D_MODEL), BF16),
        grid=(n // tm, D_MODEL // tn),
        in_specs=[
            pl.BlockSpec((tm, D_MODEL), lambda i, j: (i, 0)),
            pl.BlockSpec((tm, D_MODEL), lambda i, j: (i, 0)),
            pl.BlockSpec((D_MODEL, tn), lambda i, j: (0, j)),
            pl.BlockSpec((D_MODEL, tn), lambda i, j: (0, j)),
            pl.BlockSpec((tm, tn), lambda i, j: (i, COL_GSSD * per_d + j)),
            pl.BlockSpec((tm, tn), lambda i, j: (i, COL_GFOX * per_d + j)),
        ],
        out_specs=pl.BlockSpec((tm, tn), lambda i, j: (i, j)),
        compiler_params=_cparams(("parallel", "arbitrary")),
    )(y_ssd, o_fox, wbs, wbf, pmain, pmain)


def _outproj_kernel(u_ref, w_ref, x_ref, o_ref):
    o_ref[...] = x_ref[...] + _dot(u_ref[...], w_ref[...])


def out_proj_residual(u, w_out, x, tm=1024, tn=1024):
    n = u.shape[0]
    tm = min(tm, n)
    return pl.pallas_call(
        _outproj_kernel,
        out_shape=jax.ShapeDtypeStruct((n, D_MODEL), F32),
        grid=(n // tm, D_MODEL // tn),
        in_specs=[
            pl.BlockSpec((tm, D_MODEL), lambda i, j: (i, 0)),
            pl.BlockSpec((D_MODEL, tn), lambda i, j: (0, j)),
            pl.BlockSpec((tm, tn), lambda i, j: (i, j)),
        ],
        out_specs=pl.BlockSpec((tm, tn), lambda i, j: (i, j)),
        compiler_params=_cparams(("parallel", "arbitrary")),
    )(u, w_out, x)


def _router_kernel(x_ref, nw_ref, rw_ref, rb_ref, h_ref, idx_ref, gate_ref):
    x = x_ref[...]
    hb = (x * lax.rsqrt(jnp.mean(x * x, axis=-1, keepdims=True) + EPS) * nw_ref[...]).astype(BF16)
    h_ref[...] = hb
    work = _dot(hb, rw_ref[...]) + rb_ref[...]
    lane = lax.broadcasted_iota(jnp.int32, work.shape, 1)
    tops, idxs = [], []
    for _ in range(TOP_K):
        mx = jnp.max(work, axis=1, keepdims=True)
        ix = jnp.min(jnp.where(work == mx, lane, LANES), axis=1, keepdims=True)
        tops.append(mx)
        idxs.append(ix)
        work = jnp.where(lane == ix, -jnp.inf, work)
    es = [jnp.exp(t - tops[0]) for t in tops]
    den = es[0] + es[1] + es[2] + es[3]
    gates = jnp.zeros(work.shape, F32)
    idx = jnp.zeros(work.shape, jnp.int32)
    for k in range(TOP_K):
        gates = jnp.where(lane == k, es[k] / den, gates)
        idx = jnp.where(lane == k, idxs[k], idx)
    gate_ref[...] = gates
    idx_ref[...] = idx


def ffn_norm_router(x, norm_w, rw_pad, rb_pad, tm=512):
    n = x.shape[0]
    tm = min(tm, n)
    return pl.pallas_call(
        _router_kernel,
        out_shape=(jax.ShapeDtypeStruct((n, D_MODEL), BF16),
                   jax.ShapeDtypeStruct((n, LANES), jnp.int32),
                   jax.ShapeDtypeStruct((n, LANES), F32)),
        grid=(n // tm,),
        in_specs=[
            pl.BlockSpec((tm, D_MODEL), lambda i: (i, 0)),
            pl.BlockSpec((1, D_MODEL), lambda i: (0, 0)),
            pl.BlockSpec((D_MODEL, LANES), lambda i: (0, 0)),
            pl.BlockSpec((1, LANES), lambda i: (0, 0)),
        ],
        out_specs=(pl.BlockSpec((tm, D_MODEL), lambda i: (i, 0)),
                   pl.BlockSpec((tm, LANES), lambda i: (i, 0)),
                   pl.BlockSpec((tm, LANES), lambda i: (i, 0))),
        compiler_params=_cparams(("parallel",)),
    )(x, norm_w.reshape(1, D_MODEL), rw_pad, rb_pad)


def _moe_kernel(be_ref, nu_ref, xs_ref, wg_ref, bg_ref, wu_ref, bu_ref, wd_ref, bd_ref, y_ref):
    i = pl.program_id(0)

    @pl.when(i < nu_ref[0])
    def _():
        xb = xs_ref[...]
        a = jnp.minimum(_dot(xb, wg_ref[0]) + bg_ref[0], SWIGLU_LIMIT)
        up = jnp.clip(_dot(xb, wu_ref[0]) + bu_ref[0], -SWIGLU_LIMIT, SWIGLU_LIMIT)
        act = (up + 1.0) * (a * _sigmoid(SWIGLU_ALPHA * a))
        y = _dot(act.astype(BF16), wd_ref[0]) + bd_ref[0]
        y_ref[...] = y.astype(y_ref.dtype)

    @pl.when(i >= nu_ref[0])
    def _():
        y_ref[...] = jnp.zeros(y_ref.shape, y_ref.dtype)


def moe_experts(xs, block_expert, n_used, wg, bg, wu, bu, wd, bd):
    n_slots = xs.shape[0]
    tb = MOE_BLOCK
    emap = lambda i, be, nu: (be[i], 0, 0)
    return pl.pallas_call(
        _moe_kernel,
        out_shape=jax.ShapeDtypeStruct((n_slots, D_MODEL), BF16),
        grid_spec=pltpu.PrefetchScalarGridSpec(
            num_scalar_prefetch=2,
            grid=(n_slots // tb,),
            in_specs=[
                pl.BlockSpec((tb, D_MODEL), lambda i, be, nu: (i, 0)),
                pl.BlockSpec((1, D_MODEL, D_FF), emap),
                pl.BlockSpec((1, 1, D_FF), emap),
                pl.BlockSpec((1, D_MODEL, D_FF), emap),
                pl.BlockSpec((1, 1, D_FF), emap),
                pl.BlockSpec((1, D_FF, D_MODEL), emap),
                pl.BlockSpec((1, 1, D_MODEL), emap),
            ],
            out_specs=pl.BlockSpec((tb, D_MODEL), lambda i, be, nu: (i, 0)),
        ),
        compiler_params=_cparams(("arbitrary",)),
    )(block_expert, n_used, xs, wg, bg, wu, bu, wd, bd)


def _combine_kernel(x_ref, yk_ref, g_ref, o_ref):
    g = g_ref[...]
    acc = x_ref[...]
    for k in range(TOP_K):
        acc = acc + g[:, k:k + 1] * yk_ref[:, k * D_MODEL:(k + 1) * D_MODEL].astype(F32)
    o_ref[...] = acc


def moe_combine(x, yk, gates, tm=256):
    n = x.shape[0]
    tm = min(tm, n)
    return pl.pallas_call(
        _combine_kernel,
        out_shape=jax.ShapeDtypeStruct((n, D_MODEL), F32),
        grid=(n // tm,),
        in_specs=[
            pl.BlockSpec((tm, D_MODEL), lambda i: (i, 0)),
            pl.BlockSpec((tm, TOP_K * D_MODEL), lambda i: (i, 0)),
            pl.BlockSpec((tm, LANES), lambda i: (i, 0)),
        ],
        out_specs=pl.BlockSpec((tm, D_MODEL), lambda i: (i, 0)),
        compiler_params=_cparams(("parallel",)),
    )(x, yk, gates)


def _slot_tables(top_idx, n_tok):
    tb = MOE_BLOCK
    n_pairs = n_tok * TOP_K
    n_blocks = (n_pairs + N_EXPERTS * (tb - 1) + tb - 1) // tb
    expert_flat = top_idx.reshape(-1)
    order = jnp.argsort(expert_flat)
    sorted_expert = expert_flat[order]
    counts = jnp.bincount(expert_flat, length=N_EXPERTS)
    padded = (counts + tb - 1) // tb * tb
    padded_end = jnp.cumsum(padded)
    start = jnp.cumsum(counts) - counts
    rank = jnp.arange(n_pairs, dtype=jnp.int32) - start[sorted_expert].astype(jnp.int32)
    dest = ((padded_end - padded)[sorted_expert]).astype(jnp.int32) + rank
    slot_token = jnp.zeros((n_blocks * tb,), jnp.int32).at[dest].set((order // TOP_K).astype(jnp.int32))
    pair_slot = jnp.zeros((n_pairs,), jnp.int32).at[order].set(dest)
    block_start = jnp.arange(n_blocks, dtype=jnp.int32) * tb
    block_expert = jnp.minimum(jnp.searchsorted(padded_end, block_start, side='right'),
                               N_EXPERTS - 1).astype(jnp.int32)
    n_used = (padded_end[-1] // tb).astype(jnp.int32).reshape(1)
    return slot_token, pair_slot, block_expert, n_used


def _pack_in_proj(w_in):
    z, xbc, dtw, qkv, fw, g = jnp.split(
        w_in, [D_MODEL, D_MODEL + CONV_DIM, D_MODEL + CONV_DIM + H_SSD,
               D_MODEL + CONV_DIM + H_SSD + 3 * D_MODEL, D_MODEL + CONV_DIM + H_SSD + 3 * D_MODEL + H_FOX], axis=1)
    g_ssd, g_fox = jnp.split(g, 2, axis=1)
    w_main = jnp.concatenate([xbc, z, g_ssd, qkv, g_fox], axis=1).astype(BF16)
    pad = jnp.zeros((D_MODEL, LANES - H_FOX - H_SSD), w_in.dtype)
    w_small = jnp.concatenate([fw, dtw, pad], axis=1).astype(BF16)
    return w_main, w_small


def _pad_lanes(f_part, dt_part):
    return jnp.concatenate([f_part, dt_part, jnp.zeros((LANES - H_FOX - H_SSD,), F32)]).reshape(1, LANES)


def kernel(x, norm_mix_w, w_in, conv_w, conv_b, dt_bias, a_log, d_skip, ssd_norm_w, b_forget, w_branch_ssd,
           w_branch_fox, w_out, norm_ffn_w, router_w, router_b, w_gate, b_gate, w_up, b_up, w_down, b_down,
           norm_final_w):
    batch, seq, d = x.shape
    n = batch * seq
    depth = w_in.shape[0]
    x = x.reshape(n, d)
    for l in range(depth):
        h = rmsnorm(x, norm_mix_w[l], BF16)
        w_main, w_small = _pack_in_proj(w_in[l])
        pmain = matmul(h, w_main, BF16)
        psmall = matmul(h, w_small, F32)
        sm_bias = _pad_lanes(b_forget[l], dt_bias[l])
        alog_row = _pad_lanes(jnp.zeros((H_FOX,), F32), a_log[l])
        dskip_row = jnp.repeat(d_skip[l], SSD_HEAD_DIM).reshape(1, D_MODEL)
        y_ssd, c, ct = ssd_branch(pmain, psmall, conv_w[l], conv_b[l].reshape(1, CONV_DIM), sm_bias, alog_row,
                                  dskip_row, ssd_norm_w[l].reshape(1, D_MODEL), batch, seq)
        o_fox = fox_attention(pmain, c, ct, batch, seq)
        u = branch_merge(y_ssd, o_fox, w_branch_ssd[l].astype(BF16), w_branch_fox[l].astype(BF16), pmain)
        x = out_proj_residual(u, w_out[l].astype(BF16), x)
        rw_pad = jnp.concatenate([router_w[l], jnp.zeros((D_MODEL, LANES - N_EXPERTS), F32)], axis=1).astype(BF16)
        rb_pad = jnp.concatenate([router_b[l], jnp.full((LANES - N_EXPERTS,), NEG_BIG, F32)]).reshape(1, LANES)
        h2, idx_pad, gates_pad = ffn_norm_router(x, norm_ffn_w[l], rw_pad, rb_pad)
        slot_token, pair_slot, block_expert, n_used = _slot_tables(idx_pad[:, :TOP_K], n)
        xs = jnp.take(h2, slot_token, axis=0)
        ys = moe_experts(xs, block_expert, n_used,
                         w_gate[l].astype(BF16), b_gate[l].reshape(N_EXPERTS, 1, D_FF),
                         w_up[l].astype(BF16), b_up[l].reshape(N_EXPERTS, 1, D_FF),
                         w_down[l].astype(BF16), b_down[l].reshape(N_EXPERTS, 1, D_MODEL))
        yk = jnp.take(ys, pair_slot, axis=0).reshape(n, TOP_K * D_MODEL)
        x = moe_combine(x, yk, gates_pad)
    out = rmsnorm(x, norm_final_w, F32)
    return out.reshape(batch, seq, d)
```

```python
import functools
import math

import jax
import jax.numpy as jnp
from jax import lax
from jax.experimental import pallas as pl
from jax.experimental.pallas import tpu as pltpu

F32 = jnp.float32
BF16 = jnp.bfloat16

D_MODEL = 2048
EPS = 1e-5
SSD_HEAD_DIM = 64
H_SSD = 32
SSD_GROUPS = 8
SSD_STATE = 128
CONV_K = 4
CONV_DIM = D_MODEL + 2 * SSD_GROUPS * SSD_STATE
SSD_CHUNK = 128
FOX_HEAD_DIM = 128
H_FOX = 16
N_EXPERTS = 32
TOP_K = 4
D_FF = 768
SWIGLU_LIMIT = 7.0
SWIGLU_ALPHA = 1.702

LANES = 128
VMEM_LIMIT = 56 * 1024 * 1024
NEG_BIG = -1e30

COL_XBC, COL_Z, COL_GSSD, COL_Q, COL_K, COL_V, COL_GFOX = 0, 2, 3, 4, 5, 6, 7
N_MAIN = 8 * D_MODEL
SM_F0, SM_DT0 = 0, H_FOX

MOE_BLOCK = 512


def _cparams(sem):
    return pltpu.CompilerParams(dimension_semantics=sem, vmem_limit_bytes=VMEM_LIMIT)


def _dot(a, b):
    return jnp.dot(a, b, preferred_element_type=F32)


def _dot_nt(a, b):
    return lax.dot_general(a, b, (((1,), (1,)), ((), ())), preferred_element_type=F32)


def _sigmoid(x):
    return 1.0 / (1.0 + jnp.exp(-x))


def _rmsnorm_kernel(x_ref, w_ref, o_ref):
    x = x_ref[...]
    y = x * lax.rsqrt(jnp.mean(x * x, axis=-1, keepdims=True) + EPS) * w_ref[...]
    o_ref[...] = y.astype(o_ref.dtype)


def rmsnorm(x, w, out_dtype, tm=512):
    n, d = x.shape
    tm = min(tm, n)
    return pl.pallas_call(
        _rmsnorm_kernel,
        out_shape=jax.ShapeDtypeStruct((n, d), out_dtype),
        grid=(n // tm,),
        in_specs=[pl.BlockSpec((tm, d), lambda i: (i, 0)), pl.BlockSpec((1, d), lambda i: (0, 0))],
        out_specs=pl.BlockSpec((tm, d), lambda i: (i, 0)),
        compiler_params=_cparams(("parallel",)),
    )(x, w.reshape(1, d))


def _mm_kernel(a_ref, w_ref, o_ref):
    o_ref[...] = _dot(a_ref[...], w_ref[...]).astype(o_ref.dtype)


def matmul(a, w, out_dtype, tm=1024, tn=1024):
    m, k = a.shape
    _, n = w.shape
    tm, tn = min(tm, m), min(tn, n)
    return pl.pallas_call(
        _mm_kernel,
        out_shape=jax.ShapeDtypeStruct((m, n), out_dtype),
        grid=(m // tm, n // tn),
        in_specs=[pl.BlockSpec((tm, k), lambda i, j: (i, 0)), pl.BlockSpec((k, tn), lambda i, j: (0, j))],
        out_specs=pl.BlockSpec((tm, tn), lambda i, j: (i, j)),
        compiler_params=_cparams(("parallel", "arbitrary")),
    )(a, w)


def _cumsum_rows(val):
    L = val.shape[0]
    row = lax.broadcasted_iota(jnp.int32, (L, L), 0)
    col = lax.broadcasted_iota(jnp.int32, (L, L), 1)
    tri = jnp.where(row >= col, 1.0, 0.0).astype(BF16)
    hi = val.astype(BF16)
    r1 = val - hi.astype(F32)
    mid = r1.astype(BF16)
    lo = (r1 - mid.astype(F32)).astype(BF16)
    return _dot(tri, hi) + _dot(tri, mid) + _dot(tri, lo)


def _ssd_kernel(xbc_ref, z_ref, sm_ref, convw_ref, convb_ref, bias_ref, alog_ref, dskip_ref, normw_ref,
                y_ref, c_ref, ubuf, xc, ybuf, state, carry):
    L = SSD_CHUNK
    ci = pl.program_id(1)

    @pl.when(ci == 0)
    def _():
        ubuf[0:8, :] = jnp.zeros((8, CONV_DIM), F32)
        state[...] = jnp.zeros(state.shape, F32)
        carry[...] = jnp.zeros(carry.shape, F32)

    ubuf[8:8 + L, :] = xbc_ref[...].astype(F32)
    ct_w = 512
    for t in range(CONV_DIM // ct_w):
        sl = slice(t * ct_w, (t + 1) * ct_w)
        acc = jnp.broadcast_to(convb_ref[:, sl], (L, ct_w))
        for j in range(CONV_K):
            acc = acc + convw_ref[j:j + 1, sl] * ubuf[8 - (CONV_K - 1) + j:8 - (CONV_K - 1) + j + L, sl]
        xc[:, sl] = acc * _sigmoid(acc)
    ubuf[0:8, :] = ubuf[L:L + 8, :]

    lane = lax.broadcasted_iota(jnp.int32, (L, LANES), 1)
    lane1 = lax.broadcasted_iota(jnp.int32, (1, LANES), 1)
    v = sm_ref[...] + bias_ref[...]
    sp = jnp.log1p(jnp.exp(-jnp.abs(v)))
    logf = jnp.minimum(v, 0.0) - sp
    dt = jnp.maximum(v, 0.0) + sp
    is_dt1 = (lane1 >= SM_DT0) & (lane1 < SM_DT0 + H_SSD)
    a_row = jnp.where(is_dt1, -jnp.exp(alog_ref[...]), 0.0)
    cs = _cumsum_rows(jnp.where(lane < SM_DT0, logf, dt * a_row))
    c_full = cs + carry[...]
    c_ref[...] = c_full
    carry[...] = c_full[L - 1:L, :]

    cs_t = cs.T
    last = cs[L - 1:L, :]
    e_in = jnp.exp(cs)
    e_out = jnp.exp(last - cs)
    e_all = jnp.exp(last)
    row = lax.broadcasted_iota(jnp.int32, (L, L), 0)
    col = lax.broadcasted_iota(jnp.int32, (L, L), 1)
    causal = row >= col
    lo_half = lane < SSD_HEAD_DIM
    lo_half1 = lane1 < SSD_HEAD_DIM

    for g in range(SSD_GROUPS):
        bm = xc[:, D_MODEL + g * SSD_STATE:D_MODEL + (g + 1) * SSD_STATE]
        cm = xc[:, D_MODEL + (SSD_GROUPS + g) * SSD_STATE:D_MODEL + (SSD_GROUPS + g + 1) * SSD_STATE]
        bmb = bm.astype(BF16)
        cmb = cm.astype(BF16)
        cb = _dot_nt(cmb, bmb)
        bm_t = bm.T.astype(BF16)
        for pr in range(2):
            h0 = g * 4 + pr * 2
            k0, k1 = SM_DT0 + h0, SM_DT0 + h0 + 1
            pidx = h0 // 2

            def pair(qarr, k0=k0, k1=k1):
                return jnp.where(lo_half, qarr[:, k0:k0 + 1], qarr[:, k1:k1 + 1])

            xs_p = xc[:, h0 * SSD_HEAD_DIM:h0 * SSD_HEAD_DIM + LANES]
            xd = xs_p * pair(dt)
            xdb = xd.astype(BF16)
            ys = []
            for k in (k0, k1):
                seg = cs[:, k:k + 1] - cs_t[k:k + 1, :]
                lm = jnp.exp(jnp.where(causal, seg, NEG_BIG))
                ys.append(_dot((cb * lm).astype(BF16), xdb))
            y_diag = jnp.where(lo_half, ys[0], ys[1])
            st_prev = state[pidx]
            y_off = _dot(cmb, st_prev.astype(BF16)) * pair(e_in)
            st_new = _dot(bm_t, (xd * pair(e_out)).astype(BF16))
            e_pair = jnp.where(lo_half1, e_all[:, k0:k0 + 1], e_all[:, k1:k1 + 1])
            state[pidx] = e_pair * st_prev + st_new
            psl = slice(h0 * SSD_HEAD_DIM, h0 * SSD_HEAD_DIM + LANES)
            ybuf[:, psl] = y_diag + y_off + dskip_ref[:, psl] * xs_p

    gw = D_MODEL // SSD_GROUPS
    for g in range(SSD_GROUPS):
        sl = slice(g * gw, (g + 1) * gw)
        zz = z_ref[:, sl].astype(F32)
        gz = ybuf[:, sl] * (zz * _sigmoid(zz))
        gz = gz * lax.rsqrt(jnp.mean(gz * gz, axis=-1, keepdims=True) + EPS)
        y_ref[:, sl] = (gz * normw_ref[:, sl]).astype(y_ref.dtype)


def ssd_branch(pmain, psmall, conv_w, conv_b, sm_bias, alog_row, dskip_row, norm_w, batch, seq):
    n = batch * seq
    L = SSD_CHUNK
    nc = seq // L
    rowmap = lambda b, c: (b * nc + c, 0)
    const = lambda b, c: (0, 0)
    return pl.pallas_call(
        _ssd_kernel,
        out_shape=(jax.ShapeDtypeStruct((n, D_MODEL), BF16),
                   jax.ShapeDtypeStruct((n, LANES), F32)),
        grid=(batch, nc),
        in_specs=[
            pl.BlockSpec((L, CONV_DIM), lambda b, c: (b * nc + c, COL_XBC)),
            pl.BlockSpec((L, D_MODEL), lambda b, c: (b * nc + c, COL_Z)),
            pl.BlockSpec((L, LANES), rowmap),
            pl.BlockSpec((CONV_K, CONV_DIM), const),
            pl.BlockSpec((1, CONV_DIM), const),
            pl.BlockSpec((1, LANES), const),
            pl.BlockSpec((1, LANES), const),
            pl.BlockSpec((1, D_MODEL), const),
            pl.BlockSpec((1, D_MODEL), const),
        ],
        out_specs=(pl.BlockSpec((L, D_MODEL), rowmap),
                   pl.BlockSpec((L, LANES), rowmap)),
        scratch_shapes=[
            pltpu.VMEM((L + 8, CONV_DIM), F32),
            pltpu.VMEM((L, CONV_DIM), F32),
            pltpu.VMEM((L, D_MODEL), F32),
            pltpu.VMEM((H_SSD // 2, SSD_STATE, LANES), F32),
            pltpu.VMEM((1, LANES), F32),
        ],
        compiler_params=_cparams(("parallel", "arbitrary")),
    )(pmain, pmain, psmall, conv_w, conv_b, sm_bias, alog_row, dskip_row, norm_w)


LOG2E = math.log2(math.e)
FOX_AUG = LANES
FOX_QSCALE = LOG2E / math.sqrt(FOX_HEAD_DIM)


def _split3_lanes(vals, first):
    rows = vals.shape[0]
    hi = vals.astype(BF16).astype(F32)
    r1 = vals - hi
    mid = r1.astype(BF16).astype(F32)
    lo = (r1 - mid).astype(BF16).astype(F32)
    lane = lax.broadcasted_iota(jnp.int32, (rows, FOX_AUG), 1)
    ones = jnp.where(lane < 6, 1.0, 0.0)
    return jnp.where(lane == first, hi, jnp.where(lane == first + 1, mid, jnp.where(lane == first + 2, lo, ones)))


def _head_column(c_tile, h):
    lane = lax.broadcasted_iota(jnp.int32, c_tile.shape, 1)
    return jnp.sum(jnp.where(lane == h, c_tile, 0.0), axis=1, keepdims=True)


def _fox_kernel(q_ref, k_ref, v_ref, cq_ref, call_ref, o_ref, kaug, *, tq, seq):
    h = pl.program_id(1)
    qi = pl.program_id(2)
    rb = min(256, seq)

    @pl.when(qi == 0)
    def _():
        for r in range(seq // rb):
            rows = slice(r * rb, (r + 1) * rb)
            ck = _head_column(call_ref[rows, :], h) * LOG2E
            kaug[rows, 0:FOX_HEAD_DIM] = k_ref[rows, :]
            kaug[rows, FOX_HEAD_DIM:FOX_HEAD_DIM + FOX_AUG] = _split3_lanes(-ck, 3).astype(BF16)

    cq = _head_column(cq_ref[...], h) * LOG2E
    q_aug = jnp.concatenate([q_ref[...], _split3_lanes(cq, 0).astype(BF16)], axis=1)
    row = lax.broadcasted_iota(jnp.int32, (tq, tq), 0)
    col = lax.broadcasted_iota(jnp.int32, (tq, tq), 1)

    def step(j, carry, masked):
        m, l, acc = carry
        off = pl.multiple_of(j * tq, tq)
        s = _dot_nt(q_aug, kaug[pl.ds(off, tq), :])
        if masked:
            s = jnp.where(row >= col, s, NEG_BIG)
        m_new = jnp.maximum(m, jnp.max(s, axis=1, keepdims=True))
        p = jnp.exp2(s - m_new)
        alpha = jnp.exp2(m - m_new)
        l = alpha * l + jnp.sum(p, axis=1, keepdims=True)
        acc = alpha * acc + _dot(p.astype(BF16), v_ref[pl.ds(off, tq), :])
        return m_new, l, acc

    init = (jnp.full((tq, 1), NEG_BIG, F32), jnp.zeros((tq, 1), F32), jnp.zeros((tq, FOX_HEAD_DIM), F32))
    carry = lax.fori_loop(0, qi, lambda j, c: step(j, c, False), init)
    _, l, acc = step(qi, carry, True)
    o_ref[...] = (acc / l).astype(o_ref.dtype)


def fox_attention(pmain, c, batch, seq, tq=512):
    n = batch * seq
    tq = min(tq, seq)
    nq = seq // tq
    per_d = D_MODEL // FOX_HEAD_DIM
    return pl.pallas_call(
        functools.partial(_fox_kernel, tq=tq, seq=seq),
        out_shape=jax.ShapeDtypeStruct((n, D_MODEL), BF16),
        grid=(batch, H_FOX, nq),
        in_specs=[
            pl.BlockSpec((tq, FOX_HEAD_DIM), lambda b, h, i: (b * nq + i, COL_Q * per_d + h)),
            pl.BlockSpec((seq, FOX_HEAD_DIM), lambda b, h, i: (b, COL_K * per_d + h)),
            pl.BlockSpec((seq, FOX_HEAD_DIM), lambda b, h, i: (b, COL_V * per_d + h)),
            pl.BlockSpec((tq, LANES), lambda b, h, i: (b * nq + i, 0)),
            pl.BlockSpec((seq, LANES), lambda b, h, i: (b, 0)),
        ],
        out_specs=pl.BlockSpec((tq, FOX_HEAD_DIM), lambda b, h, i: (b * nq + i, h)),
        scratch_shapes=[pltpu.VMEM((seq, FOX_HEAD_DIM + FOX_AUG), BF16)],
        compiler_params=_cparams(("parallel", "parallel", "arbitrary")),
    )(pmain, pmain, pmain, c, c)


def _merge_kernel(y_ref, o_ref, wbs_ref, wbf_ref, gs_ref, gf_ref, u_ref):
    a = _dot(y_ref[...], wbs_ref[...])
    b = _dot(o_ref[...], wbf_ref[...])
    u = _sigmoid(gs_ref[...].astype(F32)) * a + _sigmoid(gf_ref[...].astype(F32)) * b
    u_ref[...] = u.astype(u_ref.dtype)


def branch_merge(y_ssd, o_fox, wbs, wbf, pmain, tm=512, tn=1024):
    n = y_ssd.shape[0]
    tm = min(tm, n)
    per_d = D_MODEL // tn
    return pl.pallas_call(
        _merge_kernel,
        out_shape=jax.ShapeDtypeStruct((n, D_MODEL), BF16),
        grid=(n // tm, D_MODEL // tn),
        in_specs=[
            pl.BlockSpec((tm, D_MODEL), lambda i, j: (i, 0)),
            pl.BlockSpec((tm, D_MODEL), lambda i, j: (i, 0)),
            pl.BlockSpec((D_MODEL, tn), lambda i, j: (0, j)),
            pl.BlockSpec((D_MODEL, tn), lambda i, j: (0, j)),
            pl.BlockSpec((tm, tn), lambda i, j: (i, COL_GSSD * per_d + j)),
            pl.BlockSpec((tm, tn), lambda i, j: (i, COL_GFOX * per_d + j)),
        ],
        out_specs=pl.BlockSpec((tm, tn), lambda i, j: (i, j)),
        compiler_params=_cparams(("parallel", "arbitrary")),
    )(y_ssd, o_fox, wbs, wbf, pmain, pmain)


def _outproj_kernel(u_ref, w_ref, x_ref, o_ref):
    o_ref[...] = x_ref[...] + _dot(u_ref[...], w_ref[...])


def out_proj_residual(u, w_out, x, tm=1024, tn=1024):
    n = u.shape[0]
    tm = min(tm, n)
    return pl.pallas_call(
        _outproj_kernel,
        out_shape=jax.ShapeDtypeStruct((n, D_MODEL), F32),
        grid=(n // tm, D_MODEL // tn),
        in_specs=[
            pl.BlockSpec((tm, D_MODEL), lambda i, j: (i, 0)),
            pl.BlockSpec((D_MODEL, tn), lambda i, j: (0, j)),
            pl.BlockSpec((tm, tn), lambda i, j: (i, j)),
        ],
        out_specs=pl.BlockSpec((tm, tn), lambda i, j: (i, j)),
        compiler_params=_cparams(("parallel", "arbitrary")),
    )(u, w_out, x)


def _router_kernel(x_ref, nw_ref, rw_ref, rb_ref, h_ref, idx_ref, rank_ref, gate_ref, cnt_ref, carry):
    tm = x_ref.shape[0]

    @pl.when(pl.program_id(0) == 0)
    def _():
        carry[...] = jnp.zeros(carry.shape, F32)

    x = x_ref[...]
    h = x * lax.rsqrt(jnp.mean(x * x, axis=-1, keepdims=True) + EPS) * nw_ref[...]
    h_ref[...] = h
    work = _dot(h.astype(BF16), rw_ref[...]) + rb_ref[...]
    lane = lax.broadcasted_iota(jnp.int32, work.shape, 1)
    tops, idxs = [], []
    for _ in range(TOP_K):
        mx = jnp.max(work, axis=1, keepdims=True)
        ix = jnp.min(jnp.where(work == mx, lane, LANES), axis=1, keepdims=True)
        tops.append(mx)
        idxs.append(ix)
        work = jnp.where(lane == ix, -jnp.inf, work)
    es = [jnp.exp(t - tops[0]) for t in tops]
    den = es[0] + es[1] + es[2] + es[3]

    chosen = jnp.zeros(work.shape, F32)
    for k in range(TOP_K):
        chosen = jnp.where(lane == idxs[k], 1.0, chosen)
    row = lax.broadcasted_iota(jnp.int32, (tm, tm), 0)
    col = lax.broadcasted_iota(jnp.int32, (tm, tm), 1)
    before = _dot(jnp.where(col < row, 1.0, 0.0).astype(BF16), chosen.astype(BF16)) + carry[...]
    total = before[tm - 1:tm, :] + chosen[tm - 1:tm, :]
    carry[...] = total
    cnt_ref[...] = total.astype(jnp.int32)

    gates = jnp.zeros(work.shape, F32)
    idx = jnp.zeros(work.shape, jnp.int32)
    rank = jnp.zeros(work.shape, jnp.int32)
    for k in range(TOP_K):
        rk = jnp.sum(jnp.where(lane == idxs[k], before, 0.0), axis=1, keepdims=True).astype(jnp.int32)
        gates = jnp.where(lane == k, es[k] / den, gates)
        idx = jnp.where(lane == k, idxs[k], idx)
        rank = jnp.where(lane == k, rk, rank)
    gate_ref[...] = gates
    idx_ref[...] = idx
    rank_ref[...] = rank


def ffn_norm_router(x, norm_w, rw_pad, rb_pad, tm=512):
    n = x.shape[0]
    tm = min(tm, n)
    tile = lambda i: (i, 0)
    const = lambda i: (0, 0)
    return pl.pallas_call(
        _router_kernel,
        out_shape=(jax.ShapeDtypeStruct((n, D_MODEL), F32),
                   jax.ShapeDtypeStruct((n, LANES), jnp.int32),
                   jax.ShapeDtypeStruct((n, LANES), jnp.int32),
                   jax.ShapeDtypeStruct((n, LANES), F32),
                   jax.ShapeDtypeStruct((1, LANES), jnp.int32)),
        grid=(n // tm,),
        in_specs=[
            pl.BlockSpec((tm, D_MODEL), tile),
            pl.BlockSpec((1, D_MODEL), const),
            pl.BlockSpec((D_MODEL, LANES), const),
            pl.BlockSpec((1, LANES), const),
        ],
        out_specs=(pl.BlockSpec((tm, D_MODEL), tile),
                   pl.BlockSpec((tm, LANES), tile),
                   pl.BlockSpec((tm, LANES), tile),
                   pl.BlockSpec((tm, LANES), tile),
                   pl.BlockSpec((1, LANES), const)),
        scratch_shapes=[pltpu.VMEM((1, LANES), F32)],
        compiler_params=_cparams(("arbitrary",)),
    )(x, norm_w.reshape(1, D_MODEL), rw_pad, rb_pad)


PAIRS_PER_IDX_ROW = LANES // TOP_K


def _dispatch_kernel(tab_ref, dest_ref, h_hbm, xs_hbm, zrow, sem, zsem, *, tm):
    i = pl.program_id(0)
    base = i * tm

    def issue(rr, c):
        for cc in range(LANES):
            t = base + rr * PAIRS_PER_IDX_ROW + cc // TOP_K
            pltpu.make_async_copy(h_hbm.at[pl.ds(t, 1)], xs_hbm.at[pl.ds(dest_ref[rr, cc], 1)], sem).start()
        return c

    lax.fori_loop(0, tm // PAIRS_PER_IDX_ROW, issue, 0)

    @pl.when(i == 0)
    def _():
        zrow[...] = jnp.zeros(zrow.shape, F32)

        def per_expert(e, c):
            first = tab_ref[0, e] + tab_ref[1, e]
            n_pad = tab_ref[2, e]

            def zero_row(r, c2):
                pltpu.make_async_copy(zrow.at[pl.ds(0, 1)], xs_hbm.at[pl.ds(first + r, 1)], zsem).start()
                return c2

            def wait_row(r, c2):
                pltpu.make_async_copy(zrow.at[pl.ds(0, 1)], xs_hbm.at[pl.ds(first + r, 1)], zsem).wait()
                return c2

            lax.fori_loop(0, n_pad, zero_row, 0)
            lax.fori_loop(0, n_pad, wait_row, 0)
            return c

        lax.fori_loop(0, N_EXPERTS, per_expert, 0)

        used = tab_ref[3, 0]
        n_tail = lax.shift_right_logical(xs_hbm.shape[0] - used, 3)

        def tail_copy(j):
            return pltpu.make_async_copy(zrow, xs_hbm.at[pl.ds(pl.multiple_of(used + j * 8, 8), 8)], zsem)

        lax.fori_loop(0, n_tail, lambda j, c: (tail_copy(j).start(), c)[1], 0)
        lax.fori_loop(0, n_tail, lambda j, c: (tail_copy(j).wait(), c)[1], 0)

    pltpu.make_async_copy(xs_hbm.at[pl.ds(0, TOP_K * tm)], xs_hbm.at[pl.ds(0, TOP_K * tm)], sem).wait()


def moe_dispatch(h2, dest2d, tab, n_slots, tm=512):
    n = h2.shape[0]
    tm = min(tm, n)
    return pl.pallas_call(
        functools.partial(_dispatch_kernel, tm=tm),
        out_shape=jax.ShapeDtypeStruct((n_slots, D_MODEL), F32),
        grid_spec=pltpu.PrefetchScalarGridSpec(
            num_scalar_prefetch=1,
            grid=(n // tm,),
            in_specs=[
                pl.BlockSpec((tm // PAIRS_PER_IDX_ROW, LANES), lambda i, tab: (i, 0), memory_space=pltpu.SMEM),
                pl.BlockSpec(memory_space=pl.ANY),
            ],
            out_specs=pl.BlockSpec(memory_space=pl.ANY),
            scratch_shapes=[pltpu.VMEM((8, D_MODEL), F32), pltpu.SemaphoreType.DMA(()), pltpu.SemaphoreType.DMA(())],
        ),
        compiler_params=_cparams(("arbitrary",)),
    )(tab, dest2d, h2)


def _moe_kernel(be_ref, nu_ref, xs_ref, wg_ref, bg_ref, wu_ref, bu_ref, wd_ref, bd_ref, y_ref):
    i = pl.program_id(0)

    @pl.when(i < nu_ref[0])
    def _():
        xb = xs_ref[...].astype(BF16)
        a = jnp.minimum(_dot(xb, wg_ref[0]) + bg_ref[0], SWIGLU_LIMIT)
        up = jnp.clip(_dot(xb, wu_ref[0]) + bu_ref[0], -SWIGLU_LIMIT, SWIGLU_LIMIT)
        act = (up + 1.0) * (a * _sigmoid(SWIGLU_ALPHA * a))
        y_ref[...] = _dot(act.astype(BF16), wd_ref[0]) + bd_ref[0]

    @pl.when(i >= nu_ref[0])
    def _():
        y_ref[...] = jnp.zeros(y_ref.shape, y_ref.dtype)


def moe_experts(xs, block_expert, n_used, wg, bg, wu, bu, wd, bd):
    n_slots = xs.shape[0]
    tb = MOE_BLOCK
    emap = lambda i, be, nu: (be[i], 0, 0)
    xmap = lambda i, be, nu: (jnp.minimum(i, nu[0] - 1), 0)
    return pl.pallas_call(
        _moe_kernel,
        out_shape=jax.ShapeDtypeStruct((n_slots, D_MODEL), F32),
        grid_spec=pltpu.PrefetchScalarGridSpec(
            num_scalar_prefetch=2,
            grid=(n_slots // tb,),
            in_specs=[
                pl.BlockSpec((tb, D_MODEL), xmap),
                pl.BlockSpec((1, D_MODEL, D_FF), emap),
                pl.BlockSpec((1, 1, D_FF), emap),
                pl.BlockSpec((1, D_MODEL, D_FF), emap),
                pl.BlockSpec((1, 1, D_FF), emap),
                pl.BlockSpec((1, D_FF, D_MODEL), emap),
                pl.BlockSpec((1, 1, D_MODEL), emap),
            ],
            out_specs=pl.BlockSpec((tb, D_MODEL), lambda i, be, nu: (i, 0)),
        ),
        compiler_params=_cparams(("arbitrary",)),
    )(block_expert, n_used, xs, wg, bg, wu, bu, wd, bd)


def _combine_kernel(dcur_ref, dnext_ref, x_ref, g_ref, ys_hbm, o_ref, buf, sems, *, tm):
    i = pl.program_id(0)
    n_steps = pl.num_programs(0)
    slot = i % 2

    def issue(dest_ref, s):
        def body(rr, c):
            for cc in range(LANES):
                r = rr * PAIRS_PER_IDX_ROW + cc // TOP_K
                pltpu.make_async_copy(ys_hbm.at[pl.ds(dest_ref[rr, cc], 1)],
                                      buf.at[s, cc % TOP_K, pl.ds(r, 1)], sems.at[s]).start()
            return c
        lax.fori_loop(0, tm // PAIRS_PER_IDX_ROW, body, 0)

    @pl.when(i == 0)
    def _():
        issue(dcur_ref, 0)

    @pl.when(i + 1 < n_steps)
    def _():
        issue(dnext_ref, 1 - slot)

    for k in range(TOP_K):
        pltpu.make_async_copy(ys_hbm.at[pl.ds(0, tm)], buf.at[slot, k], sems.at[slot]).wait()
    g = g_ref[...]
    acc = x_ref[...]
    for k in range(TOP_K):
        acc = acc + g[:, k:k + 1] * buf[slot, k]
    o_ref[...] = acc


def moe_combine(x, ys, dest2d, gates, tm=256):
    n = x.shape[0]
    tm = min(tm, n)
    n_steps = n // tm
    tile = lambda i: (i, 0)
    return pl.pallas_call(
        functools.partial(_combine_kernel, tm=tm),
        out_shape=jax.ShapeDtypeStruct((n, D_MODEL), F32),
        grid=(n_steps,),
        in_specs=[
            pl.BlockSpec((tm // PAIRS_PER_IDX_ROW, LANES), tile, memory_space=pltpu.SMEM),
            pl.BlockSpec((tm // PAIRS_PER_IDX_ROW, LANES), lambda i: (jnp.minimum(i + 1, n_steps - 1), 0),
                         memory_space=pltpu.SMEM),
            pl.BlockSpec((tm, D_MODEL), tile),
            pl.BlockSpec((tm, LANES), tile),
            pl.BlockSpec(memory_space=pl.ANY),
        ],
        out_specs=pl.BlockSpec((tm, D_MODEL), tile),
        scratch_shapes=[pltpu.VMEM((2, TOP_K, tm, D_MODEL), F32), pltpu.SemaphoreType.DMA((2,))],
        compiler_params=_cparams(("arbitrary",)),
    )(dest2d, dest2d, x, gates, ys)


def _slot_tables(idx, rank, counts, n_tok):
    tb = MOE_BLOCK
    n_pairs = n_tok * TOP_K
    n_blocks = (n_pairs + N_EXPERTS * (tb - 1) + tb - 1) // tb
    padded = (counts + tb - 1) // tb * tb
    padded_end = jnp.cumsum(padded)
    first = padded_end - padded
    experts = jnp.arange(N_EXPERTS, dtype=jnp.int32)
    dest = jnp.sum(jnp.where(idx[:, :, None] == experts, first, 0), axis=-1) + rank
    dest2d = dest.reshape(n_tok // PAIRS_PER_IDX_ROW, LANES)
    block_start = jnp.arange(n_blocks, dtype=jnp.int32) * tb
    block_expert = jnp.minimum(jnp.searchsorted(padded_end, block_start, side='right'),
                               N_EXPERTS - 1).astype(jnp.int32)
    n_used = (padded_end[-1] // tb).astype(jnp.int32).reshape(1)
    tab = jnp.stack([first, counts, padded - counts, jnp.broadcast_to(padded_end[-1], (N_EXPERTS,))]).astype(jnp.int32)
    return dest2d, tab, block_expert, n_used, n_blocks * tb


def _pack_in_proj(w_in):
    z, xbc, dtw, qkv, fw, g = jnp.split(
        w_in, [D_MODEL, D_MODEL + CONV_DIM, D_MODEL + CONV_DIM + H_SSD,
               D_MODEL + CONV_DIM + H_SSD + 3 * D_MODEL, D_MODEL + CONV_DIM + H_SSD + 3 * D_MODEL + H_FOX], axis=1)
    g_ssd, g_fox = jnp.split(g, 2, axis=1)
    q_w, kv_w = qkv[:, :D_MODEL] * FOX_QSCALE, qkv[:, D_MODEL:]
    w_main = jnp.concatenate([xbc, z, g_ssd, q_w, kv_w, g_fox], axis=1).astype(BF16)
    pad = jnp.zeros((D_MODEL, LANES - H_FOX - H_SSD), w_in.dtype)
    w_small = jnp.concatenate([fw, dtw, pad], axis=1).astype(BF16)
    return w_main, w_small


def _pad_lanes(f_part, dt_part):
    return jnp.concatenate([f_part, dt_part, jnp.zeros((LANES - H_FOX - H_SSD,), F32)]).reshape(1, LANES)


def kernel(x, norm_mix_w, w_in, conv_w, conv_b, dt_bias, a_log, d_skip, ssd_norm_w, b_forget, w_branch_ssd,
           w_branch_fox, w_out, norm_ffn_w, router_w, router_b, w_gate, b_gate, w_up, b_up, w_down, b_down,
           norm_final_w):
    batch, seq, d = x.shape
    n = batch * seq
    depth = w_in.shape[0]
    x = x.reshape(n, d)
    for l in range(depth):
        h = rmsnorm(x, norm_mix_w[l], BF16)
        w_main, w_small = _pack_in_proj(w_in[l])
        pmain = matmul(h, w_main, BF16)
        psmall = matmul(h, w_small, F32)
        sm_bias = _pad_lanes(b_forget[l], dt_bias[l])
        alog_row = _pad_lanes(jnp.zeros((H_FOX,), F32), a_log[l])
        dskip_row = jnp.repeat(d_skip[l], SSD_HEAD_DIM).reshape(1, D_MODEL)
        y_ssd, c = ssd_branch(pmain, psmall, conv_w[l], conv_b[l].reshape(1, CONV_DIM), sm_bias, alog_row,
                              dskip_row, ssd_norm_w[l].reshape(1, D_MODEL), batch, seq)
        o_fox = fox_attention(pmain, c, batch, seq)
        u = branch_merge(y_ssd, o_fox, w_branch_ssd[l].astype(BF16), w_branch_fox[l].astype(BF16), pmain)
        x = out_proj_residual(u, w_out[l].astype(BF16), x)
        rw_pad = jnp.concatenate([router_w[l], jnp.zeros((D_MODEL, LANES - N_EXPERTS), F32)], axis=1).astype(BF16)
        rb_pad = jnp.concatenate([router_b[l], jnp.full((LANES - N_EXPERTS,), NEG_BIG, F32)]).reshape(1, LANES)
        h2, idx_pad, rank_pad, gates_pad, counts_pad = ffn_norm_router(x, norm_ffn_w[l], rw_pad, rb_pad)
        dest2d, tab, block_expert, n_used, n_slots = _slot_tables(
            idx_pad[:, :TOP_K], rank_pad[:, :TOP_K], counts_pad[0, :N_EXPERTS], n)
        xs = moe_dispatch(h2, dest2d, tab, n_slots)
        ys = moe_experts(xs, block_expert, n_used,
                         w_gate[l].astype(BF16), b_gate[l].reshape(N_EXPERTS, 1, D_FF),
                         w_up[l].astype(BF16), b_up[l].reshape(N_EXPERTS, 1, D_FF),
                         w_down[l].astype(BF16), b_down[l].reshape(N_EXPERTS, 1, D_MODEL))
        x = moe_combine(x, ys, dest2d, gates_pad)
    out = rmsnorm(x, norm_final_w, F32)
    return out.reshape(batch, seq, d)
```

```python
import functools
import math

import jax
import jax.numpy as jnp
from jax import lax
from jax.experimental import pallas as pl
from jax.experimental.pallas import tpu as pltpu

F32 = jnp.float32
BF16 = jnp.bfloat16

D_MODEL = 2048
EPS = 1e-5
SSD_HEAD_DIM = 64
H_SSD = 32
SSD_GROUPS = 8
SSD_STATE = 128
CONV_K = 4
CONV_DIM = D_MODEL + 2 * SSD_GROUPS * SSD_STATE
SSD_CHUNK = 128
FOX_HEAD_DIM = 128
H_FOX = 16
N_EXPERTS = 32
TOP_K = 4
D_FF = 768
SWIGLU_LIMIT = 7.0
SWIGLU_ALPHA = 1.702

LANES = 128
VMEM_LIMIT = 56 * 1024 * 1024
NEG_BIG = -1e30

COL_XBC, COL_Z, COL_GSSD, COL_Q, COL_K, COL_V, COL_GFOX = 0, 2, 3, 4, 5, 6, 7
N_MAIN = 8 * D_MODEL
SM_F0, SM_DT0 = 0, H_FOX

MOE_BLOCK = 512


def _cparams(sem):
    return pltpu.CompilerParams(dimension_semantics=sem, vmem_limit_bytes=VMEM_LIMIT)


def _dot(a, b):
    return jnp.dot(a, b, preferred_element_type=F32)


def _dot_nt(a, b):
    return lax.dot_general(a, b, (((1,), (1,)), ((), ())), preferred_element_type=F32)


def _sigmoid(x):
    return 0.5 * jnp.tanh(0.5 * x) + 0.5


def _rmsnorm_kernel(x_ref, w_ref, o_ref):
    x = x_ref[...]
    y = x * lax.rsqrt(jnp.mean(x * x, axis=-1, keepdims=True) + EPS) * w_ref[...]
    o_ref[...] = y.astype(o_ref.dtype)


def rmsnorm(x, w, out_dtype, tm=512):
    n, d = x.shape
    tm = min(tm, n)
    return pl.pallas_call(
        _rmsnorm_kernel,
        out_shape=jax.ShapeDtypeStruct((n, d), out_dtype),
        grid=(n // tm,),
        in_specs=[pl.BlockSpec((tm, d), lambda i: (i, 0)), pl.BlockSpec((1, d), lambda i: (0, 0))],
        out_specs=pl.BlockSpec((tm, d), lambda i: (i, 0)),
        compiler_params=_cparams(("parallel",)),
    )(x, w.reshape(1, d))


def _mm_kernel(a_ref, w_ref, o_ref):
    o_ref[...] = _dot(a_ref[...], w_ref[...]).astype(o_ref.dtype)


def matmul(a, w, out_dtype, tm=1024, tn=1024):
    m, k = a.shape
    _, n = w.shape
    tm, tn = min(tm, m), min(tn, n)
    return pl.pallas_call(
        _mm_kernel,
        out_shape=jax.ShapeDtypeStruct((m, n), out_dtype),
        grid=(m // tm, n // tn),
        in_specs=[pl.BlockSpec((tm, k), lambda i, j: (i, 0)), pl.BlockSpec((k, tn), lambda i, j: (0, j))],
        out_specs=pl.BlockSpec((tm, tn), lambda i, j: (i, j)),
        compiler_params=_cparams(("parallel", "arbitrary")),
    )(a, w)


def _cumsum_rows(val):
    L = val.shape[0]
    row = lax.broadcasted_iota(jnp.int32, (L, L), 0)
    col = lax.broadcasted_iota(jnp.int32, (L, L), 1)
    tri = jnp.where(row >= col, 1.0, 0.0).astype(BF16)
    hi = val.astype(BF16)
    r1 = val - hi.astype(F32)
    mid = r1.astype(BF16)
    lo = (r1 - mid.astype(F32)).astype(BF16)
    return _dot(tri, hi) + _dot(tri, mid) + _dot(tri, lo)


def _ssd_kernel(xbc_ref, z_ref, sm_ref, convw_ref, convb_ref, bias_ref, alog_ref, dskip_ref, normw_ref,
                y_ref, c_ref, ubuf, xc, ybuf, state, carry):
    L = SSD_CHUNK
    ci = pl.program_id(1)

    @pl.when(ci == 0)
    def _():
        ubuf[0:8, :] = jnp.zeros((8, CONV_DIM), F32)
        state[...] = jnp.zeros(state.shape, F32)
        carry[...] = jnp.zeros(carry.shape, F32)

    ubuf[8:8 + L, :] = xbc_ref[...].astype(F32)
    ct_w = 512
    for t in range(CONV_DIM // ct_w):
        sl = slice(t * ct_w, (t + 1) * ct_w)
        acc = jnp.broadcast_to(convb_ref[:, sl], (L, ct_w))
        for j in range(CONV_K):
            acc = acc + convw_ref[j:j + 1, sl] * ubuf[8 - (CONV_K - 1) + j:8 - (CONV_K - 1) + j + L, sl]
        xc[:, sl] = acc * _sigmoid(acc)
    ubuf[0:8, :] = ubuf[L:L + 8, :]

    lane = lax.broadcasted_iota(jnp.int32, (L, LANES), 1)
    lane1 = lax.broadcasted_iota(jnp.int32, (1, LANES), 1)
    v = sm_ref[...] + bias_ref[...]
    sp = jnp.log1p(jnp.exp(-jnp.abs(v)))
    logf = jnp.minimum(v, 0.0) - sp
    dt = jnp.maximum(v, 0.0) + sp
    is_dt1 = (lane1 >= SM_DT0) & (lane1 < SM_DT0 + H_SSD)
    a_row = jnp.where(is_dt1, -jnp.exp(alog_ref[...]), 0.0)
    cs = _cumsum_rows(jnp.where(lane < SM_DT0, logf, dt * a_row))
    c_full = cs + carry[...]
    c_ref[...] = c_full
    carry[...] = c_full[L - 1:L, :]

    cs_t = cs.T
    last = cs[L - 1:L, :]
    e_in = jnp.exp(cs)
    e_out = jnp.exp(last - cs)
    e_all = jnp.exp(last)
    row = lax.broadcasted_iota(jnp.int32, (L, L), 0)
    col = lax.broadcasted_iota(jnp.int32, (L, L), 1)
    causal = row >= col
    lo_half = lane < SSD_HEAD_DIM
    lo_half1 = lane1 < SSD_HEAD_DIM

    for g in range(SSD_GROUPS):
        bm = xc[:, D_MODEL + g * SSD_STATE:D_MODEL + (g + 1) * SSD_STATE]
        cm = xc[:, D_MODEL + (SSD_GROUPS + g) * SSD_STATE:D_MODEL + (SSD_GROUPS + g + 1) * SSD_STATE]
        bmb = bm.astype(BF16)
        cmb = cm.astype(BF16)
        cb = _dot_nt(cmb, bmb)
        bm_t = bm.T.astype(BF16)
        for pr in range(2):
            h0 = g * 4 + pr * 2
            k0, k1 = SM_DT0 + h0, SM_DT0 + h0 + 1
            pidx = h0 // 2

            def pair(qarr, k0=k0, k1=k1):
                return jnp.where(lo_half, qarr[:, k0:k0 + 1], qarr[:, k1:k1 + 1])

            xs_p = xc[:, h0 * SSD_HEAD_DIM:h0 * SSD_HEAD_DIM + LANES]
            xd = xs_p * pair(dt)
            xdb = xd.astype(BF16)
            ys = []
            for k in (k0, k1):
                seg = cs[:, k:k + 1] - cs_t[k:k + 1, :]
                lm = jnp.exp(jnp.where(causal, seg, NEG_BIG))
                ys.append(_dot((cb * lm).astype(BF16), xdb))
            y_diag = jnp.where(lo_half, ys[0], ys[1])
            st_prev = state[pidx]
            y_off = _dot(cmb, st_prev.astype(BF16)) * pair(e_in)
            st_new = _dot(bm_t, (xd * pair(e_out)).astype(BF16))
            e_pair = jnp.where(lo_half1, e_all[:, k0:k0 + 1], e_all[:, k1:k1 + 1])
            state[pidx] = e_pair * st_prev + st_new
            psl = slice(h0 * SSD_HEAD_DIM, h0 * SSD_HEAD_DIM + LANES)
            ybuf[:, psl] = y_diag + y_off + dskip_ref[:, psl] * xs_p

    gw = D_MODEL // SSD_GROUPS
    for g in range(SSD_GROUPS):
        sl = slice(g * gw, (g + 1) * gw)
        zz = z_ref[:, sl].astype(F32)
        gz = ybuf[:, sl] * (zz * _sigmoid(zz))
        gz = gz * lax.rsqrt(jnp.mean(gz * gz, axis=-1, keepdims=True) + EPS)
        y_ref[:, sl] = (gz * normw_ref[:, sl]).astype(y_ref.dtype)


def ssd_branch(pmain, psmall, conv_w, conv_b, sm_bias, alog_row, dskip_row, norm_w, batch, seq):
    n = batch * seq
    L = SSD_CHUNK
    nc = seq // L
    rowmap = lambda b, c: (b * nc + c, 0)
    const = lambda b, c: (0, 0)
    return pl.pallas_call(
        _ssd_kernel,
        out_shape=(jax.ShapeDtypeStruct((n, D_MODEL), BF16),
                   jax.ShapeDtypeStruct((n, LANES), F32)),
        grid=(batch, nc),
        in_specs=[
            pl.BlockSpec((L, CONV_DIM), lambda b, c: (b * nc + c, COL_XBC)),
            pl.BlockSpec((L, D_MODEL), lambda b, c: (b * nc + c, COL_Z)),
            pl.BlockSpec((L, LANES), rowmap),
            pl.BlockSpec((CONV_K, CONV_DIM), const),
            pl.BlockSpec((1, CONV_DIM), const),
            pl.BlockSpec((1, LANES), const),
            pl.BlockSpec((1, LANES), const),
            pl.BlockSpec((1, D_MODEL), const),
            pl.BlockSpec((1, D_MODEL), const),
        ],
        out_specs=(pl.BlockSpec((L, D_MODEL), rowmap),
                   pl.BlockSpec((L, LANES), rowmap)),
        scratch_shapes=[
            pltpu.VMEM((L + 8, CONV_DIM), F32),
            pltpu.VMEM((L, CONV_DIM), F32),
            pltpu.VMEM((L, D_MODEL), F32),
            pltpu.VMEM((H_SSD // 2, SSD_STATE, LANES), F32),
            pltpu.VMEM((1, LANES), F32),
        ],
        compiler_params=_cparams(("parallel", "arbitrary")),
    )(pmain, pmain, psmall, conv_w, conv_b, sm_bias, alog_row, dskip_row, norm_w)


LOG2E = math.log2(math.e)
FOX_AUG = LANES
FOX_QSCALE = LOG2E / math.sqrt(FOX_HEAD_DIM)


def _split3_lanes(vals, first):
    rows = vals.shape[0]
    hi = vals.astype(BF16).astype(F32)
    r1 = vals - hi
    mid = r1.astype(BF16).astype(F32)
    lo = (r1 - mid).astype(BF16).astype(F32)
    lane = lax.broadcasted_iota(jnp.int32, (rows, FOX_AUG), 1)
    ones = jnp.where(lane < 6, 1.0, 0.0)
    return jnp.where(lane == first, hi, jnp.where(lane == first + 1, mid, jnp.where(lane == first + 2, lo, ones)))


def _head_column(c_tile, h):
    lane = lax.broadcasted_iota(jnp.int32, c_tile.shape, 1)
    return jnp.sum(jnp.where(lane == h, c_tile, 0.0), axis=1, keepdims=True)


def _fox_kernel(q_ref, k_ref, v_ref, cq_ref, call_ref, o_ref, kaug, *, tq, seq):
    h = pl.program_id(1)
    qi = pl.program_id(2)
    rb = min(256, seq)

    @pl.when(qi == 0)
    def _():
        for r in range(seq // rb):
            rows = slice(r * rb, (r + 1) * rb)
            ck = _head_column(call_ref[rows, :], h) * LOG2E
            kaug[rows, 0:FOX_HEAD_DIM] = k_ref[rows, :]
            kaug[rows, FOX_HEAD_DIM:FOX_HEAD_DIM + FOX_AUG] = _split3_lanes(-ck, 3).astype(BF16)

    cq = _head_column(cq_ref[...], h) * LOG2E
    q_aug = jnp.concatenate([q_ref[...], _split3_lanes(cq, 0).astype(BF16)], axis=1)
    row = lax.broadcasted_iota(jnp.int32, (tq, tq), 0)
    col = lax.broadcasted_iota(jnp.int32, (tq, tq), 1)

    def step(j, carry, masked):
        m, l, acc = carry
        off = pl.multiple_of(j * tq, tq)
        s = _dot_nt(q_aug, kaug[pl.ds(off, tq), :])
        if masked:
            s = jnp.where(row >= col, s, NEG_BIG)
        m_new = jnp.maximum(m, jnp.max(s, axis=1, keepdims=True))
        p = jnp.exp2(s - m_new)
        alpha = jnp.exp2(m - m_new)
        l = alpha * l + jnp.sum(p, axis=1, keepdims=True)
        acc = alpha * acc + _dot(p.astype(BF16), v_ref[pl.ds(off, tq), :])
        return m_new, l, acc

    init = (jnp.full((tq, 1), NEG_BIG, F32), jnp.zeros((tq, 1), F32), jnp.zeros((tq, FOX_HEAD_DIM), F32))
    carry = lax.fori_loop(0, qi, lambda j, c: step(j, c, False), init)
    _, l, acc = step(qi, carry, True)
    o_ref[...] = (acc / l).astype(o_ref.dtype)


def fox_attention(pmain, c, batch, seq, tq=512):
    n = batch * seq
    tq = min(tq, seq)
    nq = seq // tq
    per_d = D_MODEL // FOX_HEAD_DIM
    return pl.pallas_call(
        functools.partial(_fox_kernel, tq=tq, seq=seq),
        out_shape=jax.ShapeDtypeStruct((n, D_MODEL), BF16),
        grid=(batch, H_FOX, nq),
        in_specs=[
            pl.BlockSpec((tq, FOX_HEAD_DIM), lambda b, h, i: (b * nq + i, COL_Q * per_d + h)),
            pl.BlockSpec((seq, FOX_HEAD_DIM), lambda b, h, i: (b, COL_K * per_d + h)),
            pl.BlockSpec((seq, FOX_HEAD_DIM), lambda b, h, i: (b, COL_V * per_d + h)),
            pl.BlockSpec((tq, LANES), lambda b, h, i: (b * nq + i, 0)),
            pl.BlockSpec((seq, LANES), lambda b, h, i: (b, 0)),
        ],
        out_specs=pl.BlockSpec((tq, FOX_HEAD_DIM), lambda b, h, i: (b * nq + i, h)),
        scratch_shapes=[pltpu.VMEM((seq, FOX_HEAD_DIM + FOX_AUG), BF16)],
        compiler_params=_cparams(("parallel", "parallel", "arbitrary")),
    )(pmain, pmain, pmain, c, c)


def _merge_kernel(y_ref, o_ref, wbs_ref, wbf_ref, gs_ref, gf_ref, u_ref):
    a = _dot(y_ref[...], wbs_ref[...])
    b = _dot(o_ref[...], wbf_ref[...])
    u = _sigmoid(gs_ref[...].astype(F32)) * a + _sigmoid(gf_ref[...].astype(F32)) * b
    u_ref[...] = u.astype(u_ref.dtype)


def branch_merge(y_ssd, o_fox, wbs, wbf, pmain, tm=512, tn=1024):
    n = y_ssd.shape[0]
    tm = min(tm, n)
    per_d = D_MODEL // tn
    return pl.pallas_call(
        _merge_kernel,
        out_shape=jax.ShapeDtypeStruct((n, D_MODEL), BF16),
        grid=(n // tm, D_MODEL // tn),
        in_specs=[
            pl.BlockSpec((tm, D_MODEL), lambda i, j: (i, 0)),
            pl.BlockSpec((tm, D_MODEL), lambda i, j: (i, 0)),
            pl.BlockSpec((D_MODEL, tn), lambda i, j: (0, j)),
            pl.BlockSpec((D_MODEL, tn), lambda i, j: (0, j)),
            pl.BlockSpec((tm, tn), lambda i, j: (i, COL_GSSD * per_d + j)),
            pl.BlockSpec((tm, tn), lambda i, j: (i, COL_GFOX * per_d + j)),
        ],
        out_specs=pl.BlockSpec((tm, tn), lambda i, j: (i, j)),
        compiler_params=_cparams(("parallel", "arbitrary")),
    )(y_ssd, o_fox, wbs, wbf, pmain, pmain)


def _outproj_kernel(u_ref, w_ref, x_ref, o_ref):
    o_ref[...] = x_ref[...] + _dot(u_ref[...], w_ref[...])


def out_proj_residual(u, w_out, x, tm=1024, tn=1024):
    n = u.shape[0]
    tm = min(tm, n)
    return pl.pallas_call(
        _outproj_kernel,
        out_shape=jax.ShapeDtypeStruct((n, D_MODEL), F32),
        grid=(n // tm, D_MODEL // tn),
        in_specs=[
            pl.BlockSpec((tm, D_MODEL), lambda i, j: (i, 0)),
            pl.BlockSpec((D_MODEL, tn), lambda i, j: (0, j)),
            pl.BlockSpec((tm, tn), lambda i, j: (i, j)),
        ],
        out_specs=pl.BlockSpec((tm, tn), lambda i, j: (i, j)),
        compiler_params=_cparams(("parallel", "arbitrary")),
    )(u, w_out, x)


def _router_kernel(x_ref, nw_ref, rw_ref, rb_ref, h_ref, idx_ref, rank_ref, gate_ref, cnt_ref, carry):
    tm = x_ref.shape[0]

    @pl.when(pl.program_id(0) == 0)
    def _():
        carry[...] = jnp.zeros(carry.shape, F32)

    x = x_ref[...]
    h = x * lax.rsqrt(jnp.mean(x * x, axis=-1, keepdims=True) + EPS) * nw_ref[...]
    h_ref[...] = h
    work = _dot(h.astype(BF16), rw_ref[...]) + rb_ref[...]
    lane = lax.broadcasted_iota(jnp.int32, work.shape, 1)
    tops, idxs = [], []
    for _ in range(TOP_K):
        mx = jnp.max(work, axis=1, keepdims=True)
        ix = jnp.min(jnp.where(work == mx, lane, LANES), axis=1, keepdims=True)
        tops.append(mx)
        idxs.append(ix)
        work = jnp.where(lane == ix, -jnp.inf, work)
    es = [jnp.exp(t - tops[0]) for t in tops]
    den = es[0] + es[1] + es[2] + es[3]

    chosen = jnp.zeros(work.shape, F32)
    for k in range(TOP_K):
        chosen = jnp.where(lane == idxs[k], 1.0, chosen)
    row = lax.broadcasted_iota(jnp.int32, (tm, tm), 0)
    col = lax.broadcasted_iota(jnp.int32, (tm, tm), 1)
    before = _dot(jnp.where(col < row, 1.0, 0.0).astype(BF16), chosen.astype(BF16)) + carry[...]
    total = before[tm - 1:tm, :] + chosen[tm - 1:tm, :]
    carry[...] = total
    cnt_ref[...] = total.astype(jnp.int32)

    gates = jnp.zeros(work.shape, F32)
    idx = jnp.zeros(work.shape, jnp.int32)
    rank = jnp.zeros(work.shape, jnp.int32)
    for k in range(TOP_K):
        rk = jnp.sum(jnp.where(lane == idxs[k], before, 0.0), axis=1, keepdims=True).astype(jnp.int32)
        gates = jnp.where(lane == k, es[k] / den, gates)
        idx = jnp.where(lane == k, idxs[k], idx)
        rank = jnp.where(lane == k, rk, rank)
    gate_ref[...] = gates
    idx_ref[...] = idx
    rank_ref[...] = rank


def ffn_norm_router(x, norm_w, rw_pad, rb_pad, tm=512):
    n = x.shape[0]
    tm = min(tm, n)
    tile = lambda i: (i, 0)
    const = lambda i: (0, 0)
    return pl.pallas_call(
        _router_kernel,
        out_shape=(jax.ShapeDtypeStruct((n, D_MODEL), F32),
                   jax.ShapeDtypeStruct((n, LANES), jnp.int32),
                   jax.ShapeDtypeStruct((n, LANES), jnp.int32),
                   jax.ShapeDtypeStruct((n, LANES), F32),
                   jax.ShapeDtypeStruct((1, LANES), jnp.int32)),
        grid=(n // tm,),
        in_specs=[
            pl.BlockSpec((tm, D_MODEL), tile),
            pl.BlockSpec((1, D_MODEL), const),
            pl.BlockSpec((D_MODEL, LANES), const),
            pl.BlockSpec((1, LANES), const),
        ],
        out_specs=(pl.BlockSpec((tm, D_MODEL), tile),
                   pl.BlockSpec((tm, LANES), tile),
                   pl.BlockSpec((tm, LANES), tile),
                   pl.BlockSpec((tm, LANES), tile),
                   pl.BlockSpec((1, LANES), const)),
        scratch_shapes=[pltpu.VMEM((1, LANES), F32)],
        compiler_params=_cparams(("arbitrary",)),
    )(x, norm_w.reshape(1, D_MODEL), rw_pad, rb_pad)


PAIRS_PER_IDX_ROW = LANES // TOP_K


def _dispatch_kernel(tab_ref, dest_ref, h_ref, xs_hbm, zrow, sem, zsem, *, tm):
    i = pl.program_id(0)

    def issue(rr, c):
        for cc in range(LANES):
            r = rr * PAIRS_PER_IDX_ROW + cc // TOP_K
            pltpu.make_async_copy(h_ref.at[pl.ds(r, 1)], xs_hbm.at[pl.ds(dest_ref[rr, cc], 1)], sem).start()
        return c

    lax.fori_loop(0, tm // PAIRS_PER_IDX_ROW, issue, 0)

    @pl.when(i == 0)
    def _():
        zrow[...] = jnp.zeros(zrow.shape, F32)

        def per_expert(e, c):
            first = tab_ref[0, e] + tab_ref[1, e]
            n_pad = tab_ref[2, e]

            def zero_row(r, c2):
                pltpu.make_async_copy(zrow.at[pl.ds(0, 1)], xs_hbm.at[pl.ds(first + r, 1)], zsem).start()
                return c2

            def wait_row(r, c2):
                pltpu.make_async_copy(zrow.at[pl.ds(0, 1)], xs_hbm.at[pl.ds(first + r, 1)], zsem).wait()
                return c2

            lax.fori_loop(0, n_pad, zero_row, 0)
            lax.fori_loop(0, n_pad, wait_row, 0)
            return c

        lax.fori_loop(0, N_EXPERTS, per_expert, 0)

        used = tab_ref[3, 0]
        n_tail = lax.shift_right_logical(xs_hbm.shape[0] - used, 3)

        def tail_copy(j):
            return pltpu.make_async_copy(zrow, xs_hbm.at[pl.ds(pl.multiple_of(used + j * 8, 8), 8)], zsem)

        lax.fori_loop(0, n_tail, lambda j, c: (tail_copy(j).start(), c)[1], 0)
        lax.fori_loop(0, n_tail, lambda j, c: (tail_copy(j).wait(), c)[1], 0)

    pltpu.make_async_copy(xs_hbm.at[pl.ds(0, TOP_K * tm)], xs_hbm.at[pl.ds(0, TOP_K * tm)], sem).wait()


def moe_dispatch(h2, dest2d, tab, n_slots, tm=512):
    n = h2.shape[0]
    tm = min(tm, n)
    return pl.pallas_call(
        functools.partial(_dispatch_kernel, tm=tm),
        out_shape=jax.ShapeDtypeStruct((n_slots, D_MODEL), F32),
        grid_spec=pltpu.PrefetchScalarGridSpec(
            num_scalar_prefetch=1,
            grid=(n // tm,),
            in_specs=[
                pl.BlockSpec((tm // PAIRS_PER_IDX_ROW, LANES), lambda i, tab: (i, 0), memory_space=pltpu.SMEM),
                pl.BlockSpec((tm, D_MODEL), lambda i, tab: (i, 0)),
            ],
            out_specs=pl.BlockSpec(memory_space=pl.ANY),
            scratch_shapes=[pltpu.VMEM((8, D_MODEL), F32), pltpu.SemaphoreType.DMA(()), pltpu.SemaphoreType.DMA(())],
        ),
        compiler_params=_cparams(("arbitrary",)),
    )(tab, dest2d, h2)


def _moe_kernel(be_ref, nu_ref, xs_ref, wg_ref, bg_ref, wu_ref, bu_ref, wd_ref, bd_ref, y_ref):
    i = pl.program_id(0)

    @pl.when(i < nu_ref[0])
    def _():
        xb = xs_ref[...].astype(BF16)
        a = jnp.minimum(_dot(xb, wg_ref[0]) + bg_ref[0], SWIGLU_LIMIT)
        up = jnp.clip(_dot(xb, wu_ref[0]) + bu_ref[0], -SWIGLU_LIMIT, SWIGLU_LIMIT)
        act = (up + 1.0) * (a * _sigmoid(SWIGLU_ALPHA * a))
        y_ref[...] = _dot(act.astype(BF16), wd_ref[0]) + bd_ref[0]

    @pl.when(i >= nu_ref[0])
    def _():
        y_ref[...] = jnp.zeros(y_ref.shape, y_ref.dtype)


def moe_experts(xs, block_expert, n_used, wg, bg, wu, bu, wd, bd):
    n_slots = xs.shape[0]
    tb = MOE_BLOCK
    emap = lambda i, be, nu: (be[i], 0, 0)
    xmap = lambda i, be, nu: (jnp.minimum(i, nu[0] - 1), 0)
    return pl.pallas_call(
        _moe_kernel,
        out_shape=jax.ShapeDtypeStruct((n_slots, D_MODEL), F32),
        grid_spec=pltpu.PrefetchScalarGridSpec(
            num_scalar_prefetch=2,
            grid=(n_slots // tb,),
            in_specs=[
                pl.BlockSpec((tb, D_MODEL), xmap),
                pl.BlockSpec((1, D_MODEL, D_FF), emap),
                pl.BlockSpec((1, 1, D_FF), emap),
                pl.BlockSpec((1, D_MODEL, D_FF), emap),
                pl.BlockSpec((1, 1, D_FF), emap),
                pl.BlockSpec((1, D_FF, D_MODEL), emap),
                pl.BlockSpec((1, 1, D_MODEL), emap),
            ],
            out_specs=pl.BlockSpec((tb, D_MODEL), lambda i, be, nu: (i, 0)),
        ),
        compiler_params=_cparams(("arbitrary",)),
    )(block_expert, n_used, xs, wg, bg, wu, bu, wd, bd)


def _combine_kernel(dcur_ref, dnext_ref, x_ref, g_ref, ys_hbm, o_ref, buf, sems, *, tm):
    i = pl.program_id(0)
    n_steps = pl.num_programs(0)
    slot = i % 2

    def issue(dest_ref, s):
        def body(rr, c):
            for cc in range(LANES):
                r = rr * PAIRS_PER_IDX_ROW + cc // TOP_K
                pltpu.make_async_copy(ys_hbm.at[pl.ds(dest_ref[rr, cc], 1)],
                                      buf.at[s, cc % TOP_K, pl.ds(r, 1)], sems.at[s]).start()
            return c
        lax.fori_loop(0, tm // PAIRS_PER_IDX_ROW, body, 0)

    @pl.when(i == 0)
    def _():
        issue(dcur_ref, 0)

    @pl.when(i + 1 < n_steps)
    def _():
        issue(dnext_ref, 1 - slot)

    for k in range(TOP_K):
        pltpu.make_async_copy(ys_hbm.at[pl.ds(0, tm)], buf.at[slot, k], sems.at[slot]).wait()
    g = g_ref[...]
    acc = x_ref[...]
    for k in range(TOP_K):
        acc = acc + g[:, k:k + 1] * buf[slot, k]
    o_ref[...] = acc


def moe_combine(x, ys, dest2d, gates, tm=256):
    n = x.shape[0]
    tm = min(tm, n)
    n_steps = n // tm
    tile = lambda i: (i, 0)
    return pl.pallas_call(
        functools.partial(_combine_kernel, tm=tm),
        out_shape=jax.ShapeDtypeStruct((n, D_MODEL), F32),
        grid=(n_steps,),
        in_specs=[
            pl.BlockSpec((tm // PAIRS_PER_IDX_ROW, LANES), tile, memory_space=pltpu.SMEM),
            pl.BlockSpec((tm // PAIRS_PER_IDX_ROW, LANES), lambda i: (jnp.minimum(i + 1, n_steps - 1), 0),
                         memory_space=pltpu.SMEM),
            pl.BlockSpec((tm, D_MODEL), tile),
            pl.BlockSpec((tm, LANES), tile),
            pl.BlockSpec(memory_space=pl.ANY),
        ],
        out_specs=pl.BlockSpec((tm, D_MODEL), tile),
        scratch_shapes=[pltpu.VMEM((2, TOP_K, tm, D_MODEL), F32), pltpu.SemaphoreType.DMA((2,))],
        compiler_params=_cparams(("arbitrary",)),
    )(dest2d, dest2d, x, gates, ys)


def _slot_tables(idx, rank, counts, n_tok):
    tb = MOE_BLOCK
    n_pairs = n_tok * TOP_K
    n_blocks = (n_pairs + N_EXPERTS * (tb - 1) + tb - 1) // tb
    padded = (counts + tb - 1) // tb * tb
    padded_end = jnp.cumsum(padded)
    first = padded_end - padded
    experts = jnp.arange(N_EXPERTS, dtype=jnp.int32)
    dest = jnp.sum(jnp.where(idx[:, :, None] == experts, first, 0), axis=-1) + rank
    dest2d = dest.reshape(n_tok // PAIRS_PER_IDX_ROW, LANES)
    block_start = jnp.arange(n_blocks, dtype=jnp.int32) * tb
    block_expert = jnp.minimum(jnp.searchsorted(padded_end, block_start, side='right'),
                               N_EXPERTS - 1).astype(jnp.int32)
    n_used = (padded_end[-1] // tb).astype(jnp.int32).reshape(1)
    tab = jnp.stack([first, counts, padded - counts, jnp.broadcast_to(padded_end[-1], (N_EXPERTS,))]).astype(jnp.int32)
    return dest2d, tab, block_expert, n_used, n_blocks * tb


def _pack_in_proj(w_in):
    z, xbc, dtw, qkv, fw, g = jnp.split(
        w_in, [D_MODEL, D_MODEL + CONV_DIM, D_MODEL + CONV_DIM + H_SSD,
               D_MODEL + CONV_DIM + H_SSD + 3 * D_MODEL, D_MODEL + CONV_DIM + H_SSD + 3 * D_MODEL + H_FOX], axis=1)
    g_ssd, g_fox = jnp.split(g, 2, axis=1)
    q_w, kv_w = qkv[:, :D_MODEL] * FOX_QSCALE, qkv[:, D_MODEL:]
    w_main = jnp.concatenate([xbc, z, g_ssd, q_w, kv_w, g_fox], axis=1).astype(BF16)
    pad = jnp.zeros((D_MODEL, LANES - H_FOX - H_SSD), w_in.dtype)
    w_small = jnp.concatenate([fw, dtw, pad], axis=1).astype(BF16)
    return w_main, w_small


def _pad_lanes(f_part, dt_part):
    return jnp.concatenate([f_part, dt_part, jnp.zeros((LANES - H_FOX - H_SSD,), F32)]).reshape(1, LANES)


def kernel(x, norm_mix_w, w_in, conv_w, conv_b, dt_bias, a_log, d_skip, ssd_norm_w, b_forget, w_branch_ssd,
           w_branch_fox, w_out, norm_ffn_w, router_w, router_b, w_gate, b_gate, w_up, b_up, w_down, b_down,
           norm_final_w):
    batch, seq, d = x.shape
    n = batch * seq
    depth = w_in.shape[0]
    x = x.reshape(n, d)
    for l in range(depth):
        h = rmsnorm(x, norm_mix_w[l], BF16)
        w_main, w_small = _pack_in_proj(w_in[l])
        pmain = matmul(h, w_main, BF16)
        psmall = matmul(h, w_small, F32)
        sm_bias = _pad_lanes(b_forget[l], dt_bias[l])
        alog_row = _pad_lanes(jnp.zeros((H_FOX,), F32), a_log[l])
        dskip_row = jnp.repeat(d_skip[l], SSD_HEAD_DIM).reshape(1, D_MODEL)
        y_ssd, c = ssd_branch(pmain, psmall, conv_w[l], conv_b[l].reshape(1, CONV_DIM), sm_bias, alog_row,
                              dskip_row, ssd_norm_w[l].reshape(1, D_MODEL), batch, seq)
        o_fox = fox_attention(pmain, c, batch, seq)
        u = branch_merge(y_ssd, o_fox, w_branch_ssd[l].astype(BF16), w_branch_fox[l].astype(BF16), pmain)
        x = out_proj_residual(u, w_out[l].astype(BF16), x)
        rw_pad = jnp.concatenate([router_w[l], jnp.zeros((D_MODEL, LANES - N_EXPERTS), F32)], axis=1).astype(BF16)
        rb_pad = jnp.concatenate([router_b[l], jnp.full((LANES - N_EXPERTS,), NEG_BIG, F32)]).reshape(1, LANES)
        h2, idx_pad, rank_pad, gates_pad, counts_pad = ffn_norm_router(x, norm_ffn_w[l], rw_pad, rb_pad)
        dest2d, tab, block_expert, n_used, n_slots = _slot_tables(
            idx_pad[:, :TOP_K], rank_pad[:, :TOP_K], counts_pad[0, :N_EXPERTS], n)
        xs = moe_dispatch(h2, dest2d, tab, n_slots)
        ys = moe_experts(xs, block_expert, n_used,
                         w_gate[l].astype(BF16), b_gate[l].reshape(N_EXPERTS, 1, D_FF),
                         w_up[l].astype(BF16), b_up[l].reshape(N_EXPERTS, 1, D_FF),
                         w_down[l].astype(BF16), b_down[l].reshape(N_EXPERTS, 1, D_MODEL))
        x = moe_combine(x, ys, dest2d, gates_pad)
    out = rmsnorm(x, norm_final_w, F32)
    return out.reshape(batch, seq, d)
```

```python
import functools
import math

import jax
import jax.numpy as jnp
from jax import lax
from jax.experimental import pallas as pl
from jax.experimental.pallas import tpu as pltpu

F32 = jnp.float32
BF16 = jnp.bfloat16

D_MODEL = 2048
EPS = 1e-5
SSD_HEAD_DIM = 64
H_SSD = 32
SSD_GROUPS = 8
SSD_STATE = 128
CONV_K = 4
CONV_DIM = D_MODEL + 2 * SSD_GROUPS * SSD_STATE
SSD_CHUNK = 128
FOX_HEAD_DIM = 128
H_FOX = 16
N_EXPERTS = 32
TOP_K = 4
D_FF = 768
SWIGLU_LIMIT = 7.0
SWIGLU_ALPHA = 1.702

LANES = 128
VMEM_LIMIT = 56 * 1024 * 1024
NEG_BIG = -1e30

COL_XBC, COL_Z, COL_GSSD, COL_Q, COL_K, COL_V, COL_GFOX = 0, 2, 3, 4, 5, 6, 7
N_MAIN = 8 * D_MODEL
SM_F0, SM_DT0 = 0, H_FOX

MOE_BLOCK = 512


def _cparams(sem):
    return pltpu.CompilerParams(dimension_semantics=sem, vmem_limit_bytes=VMEM_LIMIT)


def _dot(a, b):
    return jnp.dot(a, b, preferred_element_type=F32)


def _dot_nt(a, b):
    return lax.dot_general(a, b, (((1,), (1,)), ((), ())), preferred_element_type=F32)


def _sigmoid(x):
    return 0.5 * jnp.tanh(0.5 * x) + 0.5


def _norm_proj_kernel(x_ref, nw_ref, w_ref, ws_ref, o_ref, os_ref, h_scr):
    @pl.when(pl.program_id(1) == 0)
    def _():
        rb = min(256, x_ref.shape[0])
        for r in range(x_ref.shape[0] // rb):
            rows = slice(r * rb, (r + 1) * rb)
            x = x_ref[rows, :]
            h = (x * lax.rsqrt(jnp.mean(x * x, axis=-1, keepdims=True) + EPS) * nw_ref[...]).astype(BF16)
            h_scr[rows, :] = h
            os_ref[rows, :] = _dot(h, ws_ref[...])

    o_ref[...] = _dot(h_scr[...], w_ref[...]).astype(o_ref.dtype)


def norm_in_proj(x, norm_w, w_main, w_small, tm=1024, tn=1024):
    n, d = x.shape
    n_main, n_small = w_main.shape[1], w_small.shape[1]
    tm = min(tm, n)
    return pl.pallas_call(
        _norm_proj_kernel,
        out_shape=(jax.ShapeDtypeStruct((n, n_main), BF16), jax.ShapeDtypeStruct((n, n_small), F32)),
        grid=(n // tm, n_main // tn),
        in_specs=[
            pl.BlockSpec((tm, d), lambda i, j: (i, 0)),
            pl.BlockSpec((1, d), lambda i, j: (0, 0)),
            pl.BlockSpec((d, tn), lambda i, j: (0, j)),
            pl.BlockSpec((d, n_small), lambda i, j: (0, 0)),
        ],
        out_specs=(pl.BlockSpec((tm, tn), lambda i, j: (i, j)),
                   pl.BlockSpec((tm, n_small), lambda i, j: (i, 0))),
        scratch_shapes=[pltpu.VMEM((tm, d), BF16)],
        compiler_params=_cparams(("parallel", "arbitrary")),
    )(x, norm_w.reshape(1, d), w_main, w_small)


def _cumsum_rows(val):
    L = val.shape[0]
    row = lax.broadcasted_iota(jnp.int32, (L, L), 0)
    col = lax.broadcasted_iota(jnp.int32, (L, L), 1)
    tri = jnp.where(row >= col, 1.0, 0.0).astype(BF16)
    hi = val.astype(BF16)
    r1 = val - hi.astype(F32)
    mid = r1.astype(BF16)
    lo = (r1 - mid.astype(F32)).astype(BF16)
    return _dot(tri, hi) + _dot(tri, mid) + _dot(tri, lo)


def _ssd_kernel(xbc_ref, z_ref, sm_ref, convw_ref, convb_ref, bias_ref, alog_ref, dskip_ref, normw_ref,
                y_ref, c_ref, ubuf, xc, ybuf, state, carry):
    L = SSD_CHUNK
    ci = pl.program_id(1)

    @pl.when(ci == 0)
    def _():
        ubuf[0:8, :] = jnp.zeros((8, CONV_DIM), F32)
        state[...] = jnp.zeros(state.shape, F32)
        carry[...] = jnp.zeros(carry.shape, F32)

    ubuf[8:8 + L, :] = xbc_ref[...].astype(F32)
    ct_w = 512
    for t in range(CONV_DIM // ct_w):
        sl = slice(t * ct_w, (t + 1) * ct_w)
        u = ubuf[:, sl]
        acc = convw_ref[0:1, sl] * u
        for j in range(1, CONV_K):
            acc = pltpu.roll(acc, 1, axis=0) + convw_ref[j:j + 1, sl] * u
        acc = acc[8:, :] + convb_ref[:, sl]
        xc[:, sl] = acc * _sigmoid(acc)
    ubuf[0:8, :] = ubuf[L:L + 8, :]

    lane = lax.broadcasted_iota(jnp.int32, (L, LANES), 1)
    lane1 = lax.broadcasted_iota(jnp.int32, (1, LANES), 1)
    v = sm_ref[...] + bias_ref[...]
    sp = jnp.log1p(jnp.exp(-jnp.abs(v)))
    logf = jnp.minimum(v, 0.0) - sp
    dt = jnp.maximum(v, 0.0) + sp
    is_dt1 = (lane1 >= SM_DT0) & (lane1 < SM_DT0 + H_SSD)
    a_row = jnp.where(is_dt1, -jnp.exp(alog_ref[...]), 0.0)
    cs = _cumsum_rows(jnp.where(lane < SM_DT0, logf, dt * a_row))
    c_full = cs + carry[...]
    c_ref[...] = c_full
    carry[...] = c_full[L - 1:L, :]

    cs_t = cs.T
    last = cs[L - 1:L, :]
    e_in = jnp.exp(cs)
    e_out = jnp.exp(last - cs)
    e_all = jnp.exp(last)
    row = lax.broadcasted_iota(jnp.int32, (L, L), 0)
    col = lax.broadcasted_iota(jnp.int32, (L, L), 1)
    causal = row >= col
    lo_half = lane < SSD_HEAD_DIM
    lo_half1 = lane1 < SSD_HEAD_DIM

    for g in range(SSD_GROUPS):
        bm = xc[:, D_MODEL + g * SSD_STATE:D_MODEL + (g + 1) * SSD_STATE]
        cm = xc[:, D_MODEL + (SSD_GROUPS + g) * SSD_STATE:D_MODEL + (SSD_GROUPS + g + 1) * SSD_STATE]
        bmb = bm.astype(BF16)
        cmb = cm.astype(BF16)
        cb = _dot_nt(cmb, bmb)
        bm_t = bm.T.astype(BF16)
        for pr in range(2):
            h0 = g * 4 + pr * 2
            k0, k1 = SM_DT0 + h0, SM_DT0 + h0 + 1
            pidx = h0 // 2

            def pair(qarr, k0=k0, k1=k1):
                return jnp.where(lo_half, qarr[:, k0:k0 + 1], qarr[:, k1:k1 + 1])

            xs_p = xc[:, h0 * SSD_HEAD_DIM:h0 * SSD_HEAD_DIM + LANES]
            xd = xs_p * pair(dt)
            xdb = xd.astype(BF16)
            ys = []
            for k in (k0, k1):
                seg = cs[:, k:k + 1] - cs_t[k:k + 1, :]
                lm = jnp.exp(jnp.where(causal, seg, NEG_BIG))
                ys.append(_dot((cb * lm).astype(BF16), xdb))
            y_diag = jnp.where(lo_half, ys[0], ys[1])
            st_prev = state[pidx]
            y_off = _dot(cmb, st_prev.astype(BF16)) * pair(e_in)
            st_new = _dot(bm_t, (xd * pair(e_out)).astype(BF16))
            e_pair = jnp.where(lo_half1, e_all[:, k0:k0 + 1], e_all[:, k1:k1 + 1])
            state[pidx] = e_pair * st_prev + st_new
            psl = slice(h0 * SSD_HEAD_DIM, h0 * SSD_HEAD_DIM + LANES)
            ybuf[:, psl] = y_diag + y_off + dskip_ref[:, psl] * xs_p

    gw = D_MODEL // SSD_GROUPS
    for g in range(SSD_GROUPS):
        sl = slice(g * gw, (g + 1) * gw)
        zz = z_ref[:, sl].astype(F32)
        gz = ybuf[:, sl] * (zz * _sigmoid(zz))
        gz = gz * lax.rsqrt(jnp.mean(gz * gz, axis=-1, keepdims=True) + EPS)
        y_ref[:, sl] = (gz * normw_ref[:, sl]).astype(y_ref.dtype)


def ssd_branch(pmain, psmall, conv_w, conv_b, sm_bias, alog_row, dskip_row, norm_w, batch, seq):
    n = batch * seq
    L = SSD_CHUNK
    nc = seq // L
    rowmap = lambda b, c: (b * nc + c, 0)
    const = lambda b, c: (0, 0)
    return pl.pallas_call(
        _ssd_kernel,
        out_shape=(jax.ShapeDtypeStruct((n, D_MODEL), BF16),
                   jax.ShapeDtypeStruct((n, LANES), F32)),
        grid=(batch, nc),
        in_specs=[
            pl.BlockSpec((L, CONV_DIM), lambda b, c: (b * nc + c, COL_XBC)),
            pl.BlockSpec((L, D_MODEL), lambda b, c: (b * nc + c, COL_Z)),
            pl.BlockSpec((L, LANES), rowmap),
            pl.BlockSpec((CONV_K, CONV_DIM), const),
            pl.BlockSpec((1, CONV_DIM), const),
            pl.BlockSpec((1, LANES), const),
            pl.BlockSpec((1, LANES), const),
            pl.BlockSpec((1, D_MODEL), const),
            pl.BlockSpec((1, D_MODEL), const),
        ],
        out_specs=(pl.BlockSpec((L, D_MODEL), rowmap),
                   pl.BlockSpec((L, LANES), rowmap)),
        scratch_shapes=[
            pltpu.VMEM((L + 8, CONV_DIM), F32),
            pltpu.VMEM((L, CONV_DIM), F32),
            pltpu.VMEM((L, D_MODEL), F32),
            pltpu.VMEM((H_SSD // 2, SSD_STATE, LANES), F32),
            pltpu.VMEM((1, LANES), F32),
        ],
        compiler_params=_cparams(("parallel", "arbitrary")),
    )(pmain, pmain, psmall, conv_w, conv_b, sm_bias, alog_row, dskip_row, norm_w)


LOG2E = math.log2(math.e)
FOX_AUG = LANES
FOX_QSCALE = LOG2E / math.sqrt(FOX_HEAD_DIM)


def _split3_lanes(vals, first):
    rows = vals.shape[0]
    hi = vals.astype(BF16).astype(F32)
    r1 = vals - hi
    mid = r1.astype(BF16).astype(F32)
    lo = (r1 - mid).astype(BF16).astype(F32)
    lane = lax.broadcasted_iota(jnp.int32, (rows, FOX_AUG), 1)
    ones = jnp.where(lane < 6, 1.0, 0.0)
    return jnp.where(lane == first, hi, jnp.where(lane == first + 1, mid, jnp.where(lane == first + 2, lo, ones)))


def _head_column(c_tile, h):
    lane = lax.broadcasted_iota(jnp.int32, c_tile.shape, 1)
    return jnp.sum(jnp.where(lane == h, c_tile, 0.0), axis=1, keepdims=True)


def _fox_kernel(q_ref, k_ref, v_ref, cq_ref, call_ref, o_ref, kaug, *, tq, seq, nh):
    hp = pl.program_id(1)
    qi = pl.program_id(2)
    rb = min(256, seq)
    hd = FOX_HEAD_DIM

    @pl.when(qi == 0)
    def _():
        for hh in range(nh):
            for r in range(seq // rb):
                rows = slice(r * rb, (r + 1) * rb)
                ck = _head_column(call_ref[rows, :], hp * nh + hh) * LOG2E
                kaug[hh, rows, 0:hd] = k_ref[rows, hh * hd:(hh + 1) * hd]
                kaug[hh, rows, hd:hd + FOX_AUG] = _split3_lanes(-ck, 3).astype(BF16)

    q_aug = []
    for hh in range(nh):
        cq = _head_column(cq_ref[...], hp * nh + hh) * LOG2E
        q_aug.append(jnp.concatenate([q_ref[:, hh * hd:(hh + 1) * hd], _split3_lanes(cq, 0).astype(BF16)], axis=1))
    row = lax.broadcasted_iota(jnp.int32, (tq, tq), 0)
    col = lax.broadcasted_iota(jnp.int32, (tq, tq), 1)

    def step(j, carry, masked):
        off = pl.multiple_of(j * tq, tq)
        out = []
        for hh in range(nh):
            m, l, acc = carry[hh]
            s = _dot_nt(q_aug[hh], kaug[hh, pl.ds(off, tq), :])
            if masked:
                s = jnp.where(row >= col, s, NEG_BIG)
            m_new = jnp.maximum(m, jnp.max(s, axis=1, keepdims=True))
            p = jnp.exp2(s - m_new)
            alpha = jnp.exp2(m - m_new)
            l = alpha * l + jnp.sum(p, axis=1, keepdims=True)
            acc = alpha * acc + _dot(p.astype(BF16), v_ref[pl.ds(off, tq), hh * hd:(hh + 1) * hd])
            out.append((m_new, l, acc))
        return tuple(out)

    init = tuple((jnp.full((tq, 1), NEG_BIG, F32), jnp.zeros((tq, 1), F32), jnp.zeros((tq, hd), F32))
                 for _ in range(nh))
    carry = lax.fori_loop(0, qi, lambda j, c: step(j, c, False), init)
    carry = step(qi, carry, True)
    for hh in range(nh):
        _, l, acc = carry[hh]
        o_ref[:, hh * hd:(hh + 1) * hd] = (acc / l).astype(o_ref.dtype)


FOX_HEADS_PER_STEP = 2


def fox_attention(pmain, c, batch, seq, tq=512):
    n = batch * seq
    tq = min(tq, seq)
    nq = seq // tq
    nh = FOX_HEADS_PER_STEP
    wb = nh * FOX_HEAD_DIM
    per_d = D_MODEL // wb
    return pl.pallas_call(
        functools.partial(_fox_kernel, tq=tq, seq=seq, nh=nh),
        out_shape=jax.ShapeDtypeStruct((n, D_MODEL), BF16),
        grid=(batch, H_FOX // nh, nq),
        in_specs=[
            pl.BlockSpec((tq, wb), lambda b, h, i: (b * nq + i, COL_Q * per_d + h)),
            pl.BlockSpec((seq, wb), lambda b, h, i: (b, COL_K * per_d + h)),
            pl.BlockSpec((seq, wb), lambda b, h, i: (b, COL_V * per_d + h)),
            pl.BlockSpec((tq, LANES), lambda b, h, i: (b * nq + i, 0)),
            pl.BlockSpec((seq, LANES), lambda b, h, i: (b, 0)),
        ],
        out_specs=pl.BlockSpec((tq, wb), lambda b, h, i: (b * nq + i, h)),
        scratch_shapes=[pltpu.VMEM((nh, seq, FOX_HEAD_DIM + FOX_AUG), BF16)],
        compiler_params=_cparams(("parallel", "parallel", "arbitrary")),
    )(pmain, pmain, pmain, c, c)


def _merge_kernel(y_ref, o_ref, wbs_ref, wbf_ref, gs_ref, gf_ref, u_ref):
    a = _dot(y_ref[...], wbs_ref[...])
    b = _dot(o_ref[...], wbf_ref[...])
    u = _sigmoid(gs_ref[...].astype(F32)) * a + _sigmoid(gf_ref[...].astype(F32)) * b
    u_ref[...] = u.astype(u_ref.dtype)


def branch_merge(y_ssd, o_fox, wbs, wbf, pmain, tm=512, tn=1024):
    n = y_ssd.shape[0]
    tm = min(tm, n)
    per_d = D_MODEL // tn
    return pl.pallas_call(
        _merge_kernel,
        out_shape=jax.ShapeDtypeStruct((n, D_MODEL), BF16),
        grid=(n // tm, D_MODEL // tn),
        in_specs=[
            pl.BlockSpec((tm, D_MODEL), lambda i, j: (i, 0)),
            pl.BlockSpec((tm, D_MODEL), lambda i, j: (i, 0)),
            pl.BlockSpec((D_MODEL, tn), lambda i, j: (0, j)),
            pl.BlockSpec((D_MODEL, tn), lambda i, j: (0, j)),
            pl.BlockSpec((tm, tn), lambda i, j: (i, COL_GSSD * per_d + j)),
            pl.BlockSpec((tm, tn), lambda i, j: (i, COL_GFOX * per_d + j)),
        ],
        out_specs=pl.BlockSpec((tm, tn), lambda i, j: (i, j)),
        compiler_params=_cparams(("parallel", "arbitrary")),
    )(y_ssd, o_fox, wbs, wbf, pmain, pmain)


def _outproj_kernel(u_ref, w_ref, x_ref, o_ref):
    o_ref[...] = x_ref[...] + _dot(u_ref[...], w_ref[...])


def out_proj_residual(u, w_out, x, tm=1024, tn=1024):
    n = u.shape[0]
    tm = min(tm, n)
    return pl.pallas_call(
        _outproj_kernel,
        out_shape=jax.ShapeDtypeStruct((n, D_MODEL), F32),
        grid=(n // tm, D_MODEL // tn),
        in_specs=[
            pl.BlockSpec((tm, D_MODEL), lambda i, j: (i, 0)),
            pl.BlockSpec((D_MODEL, tn), lambda i, j: (0, j)),
            pl.BlockSpec((tm, tn), lambda i, j: (i, j)),
        ],
        out_specs=pl.BlockSpec((tm, tn), lambda i, j: (i, j)),
        compiler_params=_cparams(("parallel", "arbitrary")),
    )(u, w_out, x)


def _router_kernel(x_ref, nw_ref, rw_ref, rb_ref, h_ref, idx_ref, rank_ref, gate_ref, cnt_ref, carry):
    tm = x_ref.shape[0]

    @pl.when(pl.program_id(0) == 0)
    def _():
        carry[...] = jnp.zeros(carry.shape, F32)

    x = x_ref[...]
    h = x * lax.rsqrt(jnp.mean(x * x, axis=-1, keepdims=True) + EPS) * nw_ref[...]
    h_ref[...] = h
    work = _dot(h.astype(BF16), rw_ref[...]) + rb_ref[...]
    lane = lax.broadcasted_iota(jnp.int32, work.shape, 1)
    tops, idxs = [], []
    for _ in range(TOP_K):
        mx = jnp.max(work, axis=1, keepdims=True)
        ix = jnp.min(jnp.where(work == mx, lane, LANES), axis=1, keepdims=True)
        tops.append(mx)
        idxs.append(ix)
        work = jnp.where(lane == ix, -jnp.inf, work)
    es = [jnp.exp(t - tops[0]) for t in tops]
    den = es[0] + es[1] + es[2] + es[3]

    chosen = jnp.zeros(work.shape, F32)
    for k in range(TOP_K):
        chosen = jnp.where(lane == idxs[k], 1.0, chosen)
    row = lax.broadcasted_iota(jnp.int32, (tm, tm), 0)
    col = lax.broadcasted_iota(jnp.int32, (tm, tm), 1)
    before = _dot(jnp.where(col < row, 1.0, 0.0).astype(BF16), chosen.astype(BF16)) + carry[...]
    total = before[tm - 1:tm, :] + chosen[tm - 1:tm, :]
    carry[...] = total
    cnt_ref[...] = total.astype(jnp.int32)

    gates = jnp.zeros(work.shape, F32)
    idx = jnp.zeros(work.shape, jnp.int32)
    rank = jnp.zeros(work.shape, jnp.int32)
    for k in range(TOP_K):
        rk = jnp.sum(jnp.where(lane == idxs[k], before, 0.0), axis=1, keepdims=True).astype(jnp.int32)
        gates = jnp.where(lane == k, es[k] / den, gates)
        idx = jnp.where(lane == k, idxs[k], idx)
        rank = jnp.where(lane == k, rk, rank)
    gate_ref[...] = gates
    idx_ref[...] = idx
    rank_ref[...] = rank


def ffn_norm_router(x, norm_w, rw_pad, rb_pad, tm=512):
    n = x.shape[0]
    tm = min(tm, n)
    tile = lambda i: (i, 0)
    const = lambda i: (0, 0)
    return pl.pallas_call(
        _router_kernel,
        out_shape=(jax.ShapeDtypeStruct((n, D_MODEL), F32),
                   jax.ShapeDtypeStruct((n, LANES), jnp.int32),
                   jax.ShapeDtypeStruct((n, LANES), jnp.int32),
                   jax.ShapeDtypeStruct((n, LANES), F32),
                   jax.ShapeDtypeStruct((1, LANES), jnp.int32)),
        grid=(n // tm,),
        in_specs=[
            pl.BlockSpec((tm, D_MODEL), tile),
            pl.BlockSpec((1, D_MODEL), const),
            pl.BlockSpec((D_MODEL, LANES), const),
            pl.BlockSpec((1, LANES), const),
        ],
        out_specs=(pl.BlockSpec((tm, D_MODEL), tile),
                   pl.BlockSpec((tm, LANES), tile),
                   pl.BlockSpec((tm, LANES), tile),
                   pl.BlockSpec((tm, LANES), tile),
                   pl.BlockSpec((1, LANES), const)),
        scratch_shapes=[pltpu.VMEM((1, LANES), F32)],
        compiler_params=_cparams(("arbitrary",)),
    )(x, norm_w.reshape(1, D_MODEL), rw_pad, rb_pad)


PAIRS_PER_IDX_ROW = LANES // TOP_K


def _dispatch_kernel(tab_ref, dest_ref, h_ref, xs_hbm, zrow, sem, zsem, *, tm):
    i = pl.program_id(0)

    def issue(rr, c):
        for cc in range(LANES):
            r = rr * PAIRS_PER_IDX_ROW + cc // TOP_K
            pltpu.make_async_copy(h_ref.at[pl.ds(r, 1)], xs_hbm.at[pl.ds(dest_ref[rr, cc], 1)], sem).start(
                priority=cc % 2)
        return c

    lax.fori_loop(0, tm // PAIRS_PER_IDX_ROW, issue, 0)

    @pl.when(i == 0)
    def _():
        zrow[...] = jnp.zeros(zrow.shape, F32)

        def per_expert(e, c):
            first = tab_ref[0, e] + tab_ref[1, e]
            n_pad = tab_ref[2, e]

            def zero_row(r, c2):
                pltpu.make_async_copy(zrow.at[pl.ds(0, 1)], xs_hbm.at[pl.ds(first + r, 1)], zsem).start()
                return c2

            def wait_row(r, c2):
                pltpu.make_async_copy(zrow.at[pl.ds(0, 1)], xs_hbm.at[pl.ds(first + r, 1)], zsem).wait()
                return c2

            lax.fori_loop(0, n_pad, zero_row, 0)
            lax.fori_loop(0, n_pad, wait_row, 0)
            return c

        lax.fori_loop(0, N_EXPERTS, per_expert, 0)

        used = tab_ref[3, 0]
        n_tail = lax.shift_right_logical(xs_hbm.shape[0] - used, 3)

        def tail_copy(j):
            return pltpu.make_async_copy(zrow, xs_hbm.at[pl.ds(pl.multiple_of(used + j * 8, 8), 8)], zsem)

        lax.fori_loop(0, n_tail, lambda j, c: (tail_copy(j).start(), c)[1], 0)
        lax.fori_loop(0, n_tail, lambda j, c: (tail_copy(j).wait(), c)[1], 0)

    pltpu.make_async_copy(xs_hbm.at[pl.ds(0, TOP_K * tm)], xs_hbm.at[pl.ds(0, TOP_K * tm)], sem).wait()


def moe_dispatch(h2, dest2d, tab, n_slots, tm=512):
    n = h2.shape[0]
    tm = min(tm, n)
    return pl.pallas_call(
        functools.partial(_dispatch_kernel, tm=tm),
        out_shape=jax.ShapeDtypeStruct((n_slots, D_MODEL), F32),
        grid_spec=pltpu.PrefetchScalarGridSpec(
            num_scalar_prefetch=1,
            grid=(n // tm,),
            in_specs=[
                pl.BlockSpec((tm // PAIRS_PER_IDX_ROW, LANES), lambda i, tab: (i, 0), memory_space=pltpu.SMEM),
                pl.BlockSpec((tm, D_MODEL), lambda i, tab: (i, 0)),
            ],
            out_specs=pl.BlockSpec(memory_space=pl.ANY),
            scratch_shapes=[pltpu.VMEM((8, D_MODEL), F32), pltpu.SemaphoreType.DMA(()), pltpu.SemaphoreType.DMA(())],
        ),
        compiler_params=_cparams(("arbitrary",)),
    )(tab, dest2d, h2)


def _moe_kernel(be_ref, nu_ref, xs_ref, wg_ref, bg_ref, wu_ref, bu_ref, wd_ref, bd_ref, y_ref):
    i = pl.program_id(0)

    @pl.when(i < nu_ref[0])
    def _():
        xb = xs_ref[...].astype(BF16)
        a = jnp.minimum(_dot(xb, wg_ref[0]) + bg_ref[0], SWIGLU_LIMIT)
        up = jnp.clip(_dot(xb, wu_ref[0]) + bu_ref[0], -SWIGLU_LIMIT, SWIGLU_LIMIT)
        act = (up + 1.0) * (a * _sigmoid(SWIGLU_ALPHA * a))
        y_ref[...] = _dot(act.astype(BF16), wd_ref[0]) + bd_ref[0]

    @pl.when(i >= nu_ref[0])
    def _():
        y_ref[...] = jnp.zeros(y_ref.shape, y_ref.dtype)


def moe_experts(xs, block_expert, n_used, wg, bg, wu, bu, wd, bd):
    n_slots = xs.shape[0]
    tb = MOE_BLOCK
    emap = lambda i, be, nu: (be[i], 0, 0)
    xmap = lambda i, be, nu: (jnp.minimum(i, nu[0] - 1), 0)
    return pl.pallas_call(
        _moe_kernel,
        out_shape=jax.ShapeDtypeStruct((n_slots, D_MODEL), F32),
        grid_spec=pltpu.PrefetchScalarGridSpec(
            num_scalar_prefetch=2,
            grid=(n_slots // tb,),
            in_specs=[
                pl.BlockSpec((tb, D_MODEL), xmap),
                pl.BlockSpec((1, D_MODEL, D_FF), emap),
                pl.BlockSpec((1, 1, D_FF), emap),
                pl.BlockSpec((1, D_MODEL, D_FF), emap),
                pl.BlockSpec((1, 1, D_FF), emap),
                pl.BlockSpec((1, D_FF, D_MODEL), emap),
                pl.BlockSpec((1, 1, D_MODEL), emap),
            ],
            out_specs=pl.BlockSpec((tb, D_MODEL), lambda i, be, nu: (i, 0)),
        ),
        compiler_params=_cparams(("arbitrary",)),
    )(block_expert, n_used, xs, wg, bg, wu, bu, wd, bd)


def _combine_kernel(dcur_ref, dnext_ref, x_ref, g_ref, nw_ref, ys_hbm, o_ref, buf, sems, *, tm, final_norm):
    i = pl.program_id(0)
    n_steps = pl.num_programs(0)
    slot = i % 2

    def issue(dest_ref, s):
        def body(rr, c):
            for cc in range(LANES):
                r = rr * PAIRS_PER_IDX_ROW + cc // TOP_K
                pltpu.make_async_copy(ys_hbm.at[pl.ds(dest_ref[rr, cc], 1)],
                                      buf.at[s, cc % TOP_K, pl.ds(r, 1)], sems.at[s]).start(priority=cc % 2)
            return c
        lax.fori_loop(0, tm // PAIRS_PER_IDX_ROW, body, 0)

    @pl.when(i == 0)
    def _():
        issue(dcur_ref, 0)

    @pl.when(i + 1 < n_steps)
    def _():
        issue(dnext_ref, 1 - slot)

    for k in range(TOP_K):
        pltpu.make_async_copy(ys_hbm.at[pl.ds(0, tm)], buf.at[slot, k], sems.at[slot]).wait()
    g = g_ref[...]
    acc = x_ref[...]
    for k in range(TOP_K):
        acc = acc + g[:, k:k + 1] * buf[slot, k]
    if final_norm:
        acc = acc * lax.rsqrt(jnp.mean(acc * acc, axis=-1, keepdims=True) + EPS) * nw_ref[...]
    o_ref[...] = acc


def moe_combine(x, ys, dest2d, gates, norm_w, final_norm, tm=256):
    n = x.shape[0]
    tm = min(tm, n)
    n_steps = n // tm
    tile = lambda i: (i, 0)
    return pl.pallas_call(
        functools.partial(_combine_kernel, tm=tm, final_norm=final_norm),
        out_shape=jax.ShapeDtypeStruct((n, D_MODEL), F32),
        grid=(n_steps,),
        in_specs=[
            pl.BlockSpec((tm // PAIRS_PER_IDX_ROW, LANES), tile, memory_space=pltpu.SMEM),
            pl.BlockSpec((tm // PAIRS_PER_IDX_ROW, LANES), lambda i: (jnp.minimum(i + 1, n_steps - 1), 0),
                         memory_space=pltpu.SMEM),
            pl.BlockSpec((tm, D_MODEL), tile),
            pl.BlockSpec((tm, LANES), tile),
            pl.BlockSpec((1, D_MODEL), lambda i: (0, 0)),
            pl.BlockSpec(memory_space=pl.ANY),
        ],
        out_specs=pl.BlockSpec((tm, D_MODEL), tile),
        scratch_shapes=[pltpu.VMEM((2, TOP_K, tm, D_MODEL), F32), pltpu.SemaphoreType.DMA((2,))],
        compiler_params=_cparams(("arbitrary",)),
    )(dest2d, dest2d, x, gates, norm_w.reshape(1, D_MODEL), ys)


def _slot_tables(idx, rank, counts, n_tok):
    tb = MOE_BLOCK
    n_pairs = n_tok * TOP_K
    n_blocks = (n_pairs + N_EXPERTS * (tb - 1) + tb - 1) // tb
    padded = (counts + tb - 1) // tb * tb
    padded_end = jnp.cumsum(padded)
    first = padded_end - padded
    experts = jnp.arange(N_EXPERTS, dtype=jnp.int32)
    dest = jnp.sum(jnp.where(idx[:, :, None] == experts, first, 0), axis=-1) + rank
    dest2d = dest.reshape(n_tok // PAIRS_PER_IDX_ROW, LANES)
    block_start = jnp.arange(n_blocks, dtype=jnp.int32) * tb
    block_expert = jnp.minimum(jnp.sum(padded_end[None, :] <= block_start[:, None], axis=1),
                               N_EXPERTS - 1).astype(jnp.int32)
    n_used = (padded_end[-1] // tb).astype(jnp.int32).reshape(1)
    tab = jnp.stack([first, counts, padded - counts, jnp.broadcast_to(padded_end[-1], (N_EXPERTS,))]).astype(jnp.int32)
    return dest2d, tab, block_expert, n_used, n_blocks * tb


def _pack_in_proj(w_in):
    z, xbc, dtw, qkv, fw, g = jnp.split(
        w_in, [D_MODEL, D_MODEL + CONV_DIM, D_MODEL + CONV_DIM + H_SSD,
               D_MODEL + CONV_DIM + H_SSD + 3 * D_MODEL, D_MODEL + CONV_DIM + H_SSD + 3 * D_MODEL + H_FOX], axis=1)
    g_ssd, g_fox = jnp.split(g, 2, axis=1)
    q_w, kv_w = qkv[:, :D_MODEL] * FOX_QSCALE, qkv[:, D_MODEL:]
    w_main = jnp.concatenate([xbc, z, g_ssd, q_w, kv_w, g_fox], axis=1).astype(BF16)
    pad = jnp.zeros((D_MODEL, LANES - H_FOX - H_SSD), w_in.dtype)
    w_small = jnp.concatenate([fw, dtw, pad], axis=1).astype(BF16)
    return w_main, w_small


def _pad_lanes(f_part, dt_part):
    return jnp.concatenate([f_part, dt_part, jnp.zeros((LANES - H_FOX - H_SSD,), F32)]).reshape(1, LANES)


def kernel(x, norm_mix_w, w_in, conv_w, conv_b, dt_bias, a_log, d_skip, ssd_norm_w, b_forget, w_branch_ssd,
           w_branch_fox, w_out, norm_ffn_w, router_w, router_b, w_gate, b_gate, w_up, b_up, w_down, b_down,
           norm_final_w):
    batch, seq, d = x.shape
    n = batch * seq
    depth = w_in.shape[0]
    x = x.reshape(n, d)
    for l in range(depth):
        w_main, w_small = _pack_in_proj(w_in[l])
        pmain, psmall = norm_in_proj(x, norm_mix_w[l], w_main, w_small)
        sm_bias = _pad_lanes(b_forget[l], dt_bias[l])
        alog_row = _pad_lanes(jnp.zeros((H_FOX,), F32), a_log[l])
        dskip_row = jnp.repeat(d_skip[l], SSD_HEAD_DIM).reshape(1, D_MODEL)
        y_ssd, c = ssd_branch(pmain, psmall, conv_w[l], conv_b[l].reshape(1, CONV_DIM), sm_bias, alog_row,
                              dskip_row, ssd_norm_w[l].reshape(1, D_MODEL), batch, seq)
        o_fox = fox_attention(pmain, c, batch, seq)
        u = branch_merge(y_ssd, o_fox, w_branch_ssd[l].astype(BF16), w_branch_fox[l].astype(BF16), pmain)
        x = out_proj_residual(u, w_out[l].astype(BF16), x)
        rw_pad = jnp.concatenate([router_w[l], jnp.zeros((D_MODEL, LANES - N_EXPERTS), F32)], axis=1).astype(BF16)
        rb_pad = jnp.concatenate([router_b[l], jnp.full((LANES - N_EXPERTS,), NEG_BIG, F32)]).reshape(1, LANES)
        h2, idx_pad, rank_pad, gates_pad, counts_pad = ffn_norm_router(x, norm_ffn_w[l], rw_pad, rb_pad)
        dest2d, tab, block_expert, n_used, n_slots = _slot_tables(
            idx_pad[:, :TOP_K], rank_pad[:, :TOP_K], counts_pad[0, :N_EXPERTS], n)
        xs = moe_dispatch(h2, dest2d, tab, n_slots)
        ys = moe_experts(xs, block_expert, n_used,
                         w_gate[l].astype(BF16), b_gate[l].reshape(N_EXPERTS, 1, D_FF),
                         w_up[l].astype(BF16), b_up[l].reshape(N_EXPERTS, 1, D_FF),
                         w_down[l].astype(BF16), b_down[l].reshape(N_EXPERTS, 1, D_MODEL))
        x = moe_combine(x, ys, dest2d, gates_pad, norm_final_w, final_norm=(l == depth - 1))
    return x.reshape(batch, seq, d)
```

```python
import functools
import math

import jax
import jax.numpy as jnp
from jax import lax
from jax.experimental import pallas as pl
from jax.experimental.pallas import tpu as pltpu

F32 = jnp.float32
BF16 = jnp.bfloat16

D_MODEL = 2048
EPS = 1e-5
SSD_HEAD_DIM = 64
H_SSD = 32
SSD_GROUPS = 8
SSD_STATE = 128
CONV_K = 4
CONV_DIM = D_MODEL + 2 * SSD_GROUPS * SSD_STATE
SSD_CHUNK = 128
FOX_HEAD_DIM = 128
H_FOX = 16
N_EXPERTS = 32
TOP_K = 4
D_FF = 768
SWIGLU_LIMIT = 7.0
SWIGLU_ALPHA = 1.702

LANES = 128
VMEM_LIMIT = 56 * 1024 * 1024
NEG_BIG = -1e30

COL_XBC, COL_Z, COL_GSSD, COL_Q, COL_K, COL_V, COL_GFOX = 0, 2, 3, 4, 5, 6, 7
SM_DT0 = H_FOX

MOE_BLOCK = 512


def _cparams(sem):
    return pltpu.CompilerParams(dimension_semantics=sem, vmem_limit_bytes=VMEM_LIMIT)


def _dot(a, b):
    return jnp.dot(a, b, preferred_element_type=F32)


def _dot_nt(a, b):
    return lax.dot_general(a, b, (((1,), (1,)), ((), ())), preferred_element_type=F32)


def _sigmoid(x):
    return 0.5 * jnp.tanh(0.5 * x) + 0.5


def _norm_proj_kernel(x_ref, nw_ref, w_ref, ws_ref, o_ref, os_ref, h_scr):
    @pl.when(pl.program_id(1) == 0)
    def _():
        rb = min(256, x_ref.shape[0])
        for r in range(x_ref.shape[0] // rb):
            rows = slice(r * rb, (r + 1) * rb)
            x = x_ref[rows, :]
            h = (x * lax.rsqrt(jnp.mean(x * x, axis=-1, keepdims=True) + EPS) * nw_ref[...]).astype(BF16)
            h_scr[rows, :] = h
            os_ref[rows, :] = _dot(h, ws_ref[...])

    o_ref[...] = _dot(h_scr[...], w_ref[...]).astype(o_ref.dtype)


def norm_in_proj(x, norm_w, w_main, w_small, tm=1024, tn=1024):
    n, d = x.shape
    n_main, n_small = w_main.shape[1], w_small.shape[1]
    tm = min(tm, n)
    return pl.pallas_call(
        _norm_proj_kernel,
        out_shape=(jax.ShapeDtypeStruct((n, n_main), BF16), jax.ShapeDtypeStruct((n, n_small), F32)),
        grid=(n // tm, n_main // tn),
        in_specs=[
            pl.BlockSpec((tm, d), lambda i, j: (i, 0)),
            pl.BlockSpec((1, d), lambda i, j: (0, 0)),
            pl.BlockSpec((d, tn), lambda i, j: (0, j)),
            pl.BlockSpec((d, n_small), lambda i, j: (0, 0)),
        ],
        out_specs=(pl.BlockSpec((tm, tn), lambda i, j: (i, j)),
                   pl.BlockSpec((tm, n_small), lambda i, j: (i, 0))),
        scratch_shapes=[pltpu.VMEM((tm, d), BF16)],
        compiler_params=_cparams(("parallel", "arbitrary")),
    )(x, norm_w.reshape(1, d), w_main, w_small)


def _cumsum_rows(val):
    L = val.shape[0]
    row = lax.broadcasted_iota(jnp.int32, (L, L), 0)
    col = lax.broadcasted_iota(jnp.int32, (L, L), 1)
    tri = jnp.where(row >= col, 1.0, 0.0).astype(BF16)
    hi = val.astype(BF16)
    r1 = val - hi.astype(F32)
    mid = r1.astype(BF16)
    lo = (r1 - mid.astype(F32)).astype(BF16)
    return _dot(tri, hi) + _dot(tri, mid) + _dot(tri, lo)


def _ssd_kernel(xbc_ref, z_ref, sm_ref, convw_ref, convb_ref, bias_ref, alog_ref, dskip_ref, normw_ref,
                y_ref, c_ref, ubuf, xc, ybuf, state, carry):
    L = SSD_CHUNK
    ci = pl.program_id(1)

    @pl.when(ci == 0)
    def _():
        ubuf[0:8, :] = jnp.zeros((8, CONV_DIM), F32)
        state[...] = jnp.zeros(state.shape, F32)
        carry[...] = jnp.zeros(carry.shape, F32)

    ubuf[8:8 + L, :] = xbc_ref[...].astype(F32)
    ct_w = 512
    for t in range(CONV_DIM // ct_w):
        sl = slice(t * ct_w, (t + 1) * ct_w)
        u = ubuf[:, sl]
        u1 = pltpu.roll(u, 1, axis=0)
        near = convw_ref[3:4, sl] * u + convw_ref[2:3, sl] * u1
        far = convw_ref[1:2, sl] * u + convw_ref[0:1, sl] * u1
        acc = (near + pltpu.roll(far, 2, axis=0))[8:, :] + convb_ref[:, sl]
        xc[:, sl] = acc * _sigmoid(acc)
    ubuf[0:8, :] = ubuf[L:L + 8, :]

    lane = lax.broadcasted_iota(jnp.int32, (L, LANES), 1)
    lane1 = lax.broadcasted_iota(jnp.int32, (1, LANES), 1)
    v = sm_ref[...] + bias_ref[...]
    sp = jnp.log1p(jnp.exp(-jnp.abs(v)))
    logf = jnp.minimum(v, 0.0) - sp
    dt = jnp.maximum(v, 0.0) + sp
    is_dt1 = (lane1 >= SM_DT0) & (lane1 < SM_DT0 + H_SSD)
    a_row = jnp.where(is_dt1, -jnp.exp(alog_ref[...]), 0.0)
    cs = _cumsum_rows(jnp.where(lane < SM_DT0, logf, dt * a_row))
    c_full = cs + carry[...]
    c_ref[...] = c_full
    carry[...] = c_full[L - 1:L, :]

    cs_t = cs.T
    last = cs[L - 1:L, :]
    e_in = jnp.exp(cs)
    e_out = jnp.exp(last - cs)
    e_all = jnp.exp(last)
    row = lax.broadcasted_iota(jnp.int32, (L, L), 0)
    col = lax.broadcasted_iota(jnp.int32, (L, L), 1)
    causal = row >= col
    lo_half = lane < SSD_HEAD_DIM
    lo_half1 = lane1 < SSD_HEAD_DIM

    for g in range(SSD_GROUPS):
        bm = xc[:, D_MODEL + g * SSD_STATE:D_MODEL + (g + 1) * SSD_STATE]
        cm = xc[:, D_MODEL + (SSD_GROUPS + g) * SSD_STATE:D_MODEL + (SSD_GROUPS + g + 1) * SSD_STATE]
        bmb = bm.astype(BF16)
        cmb = cm.astype(BF16)
        cb = _dot_nt(cmb, bmb)
        bm_t = bm.T.astype(BF16)
        for pr in range(2):
            h0 = g * 4 + pr * 2
            k0, k1 = SM_DT0 + h0, SM_DT0 + h0 + 1
            pidx = h0 // 2

            def pair(qarr, k0=k0, k1=k1):
                return jnp.where(lo_half, qarr[:, k0:k0 + 1], qarr[:, k1:k1 + 1])

            xs_p = xc[:, h0 * SSD_HEAD_DIM:h0 * SSD_HEAD_DIM + LANES]
            xd = xs_p * pair(dt)
            xdb = xd.astype(BF16)
            ys = []
            for k in (k0, k1):
                seg = cs[:, k:k + 1] - cs_t[k:k + 1, :]
                lm = jnp.exp(jnp.where(causal, seg, NEG_BIG))
                ys.append(_dot((cb * lm).astype(BF16), xdb))
            y_diag = jnp.where(lo_half, ys[0], ys[1])
            st_prev = state[pidx]
            y_off = _dot(cmb, st_prev.astype(BF16)) * pair(e_in)
            st_new = _dot(bm_t, (xd * pair(e_out)).astype(BF16))
            e_pair = jnp.where(lo_half1, e_all[:, k0:k0 + 1], e_all[:, k1:k1 + 1])
            state[pidx] = e_pair * st_prev + st_new
            psl = slice(h0 * SSD_HEAD_DIM, h0 * SSD_HEAD_DIM + LANES)
            ybuf[:, psl] = y_diag + y_off + dskip_ref[:, psl] * xs_p

    gw = D_MODEL // SSD_GROUPS
    for g in range(SSD_GROUPS):
        sl = slice(g * gw, (g + 1) * gw)
        zz = z_ref[:, sl].astype(F32)
        gz = ybuf[:, sl] * (zz * _sigmoid(zz))
        gz = gz * lax.rsqrt(jnp.mean(gz * gz, axis=-1, keepdims=True) + EPS)
        y_ref[:, sl] = (gz * normw_ref[:, sl]).astype(y_ref.dtype)


def ssd_branch(pmain, psmall, conv_w, conv_b, sm_bias, alog_row, dskip_row, norm_w, batch, seq):
    n = batch * seq
    L = SSD_CHUNK
    nc = seq // L
    rowmap = lambda b, c: (b * nc + c, 0)
    const = lambda b, c: (0, 0)
    return pl.pallas_call(
        _ssd_kernel,
        out_shape=(jax.ShapeDtypeStruct((n, D_MODEL), BF16),
                   jax.ShapeDtypeStruct((n, LANES), F32)),
        grid=(batch, nc),
        in_specs=[
            pl.BlockSpec((L, CONV_DIM), lambda b, c: (b * nc + c, COL_XBC)),
            pl.BlockSpec((L, D_MODEL), lambda b, c: (b * nc + c, COL_Z)),
            pl.BlockSpec((L, LANES), rowmap),
            pl.BlockSpec((CONV_K, CONV_DIM), const),
            pl.BlockSpec((1, CONV_DIM), const),
            pl.BlockSpec((1, LANES), const),
            pl.BlockSpec((1, LANES), const),
            pl.BlockSpec((1, D_MODEL), const),
            pl.BlockSpec((1, D_MODEL), const),
        ],
        out_specs=(pl.BlockSpec((L, D_MODEL), rowmap),
                   pl.BlockSpec((L, LANES), rowmap)),
        scratch_shapes=[
            pltpu.VMEM((L + 8, CONV_DIM), F32),
            pltpu.VMEM((L, CONV_DIM), F32),
            pltpu.VMEM((L, D_MODEL), F32),
            pltpu.VMEM((H_SSD // 2, SSD_STATE, LANES), F32),
            pltpu.VMEM((1, LANES), F32),
        ],
        compiler_params=_cparams(("parallel", "arbitrary")),
    )(pmain, pmain, psmall, conv_w, conv_b, sm_bias, alog_row, dskip_row, norm_w)


LOG2E = math.log2(math.e)
FOX_AUG = LANES
FOX_QSCALE = LOG2E / math.sqrt(FOX_HEAD_DIM)


def _split3_lanes(vals, first):
    rows = vals.shape[0]
    hi = vals.astype(BF16).astype(F32)
    r1 = vals - hi
    mid = r1.astype(BF16).astype(F32)
    lo = (r1 - mid).astype(BF16).astype(F32)
    lane = lax.broadcasted_iota(jnp.int32, (rows, FOX_AUG), 1)
    ones = jnp.where(lane < 6, 1.0, 0.0)
    return jnp.where(lane == first, hi, jnp.where(lane == first + 1, mid, jnp.where(lane == first + 2, lo, ones)))


def _head_column(c_tile, h):
    lane = lax.broadcasted_iota(jnp.int32, c_tile.shape, 1)
    return jnp.sum(jnp.where(lane == h, c_tile, 0.0), axis=1, keepdims=True)


def _fox_kernel(q_ref, k_ref, v_ref, cq_ref, call_ref, o_ref, kaug, *, tq, seq, nh):
    hp = pl.program_id(1)
    qi = pl.program_id(2)
    rb = min(256, seq)
    hd = FOX_HEAD_DIM

    @pl.when(qi == 0)
    def _():
        for hh in range(nh):
            for r in range(seq // rb):
                rows = slice(r * rb, (r + 1) * rb)
                ck = _head_column(call_ref[rows, :], hp * nh + hh) * LOG2E
                kaug[hh, rows, 0:hd] = k_ref[rows, hh * hd:(hh + 1) * hd]
                kaug[hh, rows, hd:hd + FOX_AUG] = _split3_lanes(-ck, 3).astype(BF16)

    q_aug = []
    for hh in range(nh):
        cq = _head_column(cq_ref[...], hp * nh + hh) * LOG2E
        q_aug.append(jnp.concatenate([q_ref[:, hh * hd:(hh + 1) * hd], _split3_lanes(cq, 0).astype(BF16)], axis=1))
    row = lax.broadcasted_iota(jnp.int32, (tq, tq), 0)
    col = lax.broadcasted_iota(jnp.int32, (tq, tq), 1)

    def step(j, carry, masked):
        off = pl.multiple_of(j * tq, tq)
        out = []
        for hh in range(nh):
            m, l, acc = carry[hh]
            s = _dot_nt(q_aug[hh], kaug[hh, pl.ds(off, tq), :])
            if masked:
                s = jnp.where(row >= col, s, NEG_BIG)
            m_new = jnp.maximum(m, jnp.max(s, axis=1, keepdims=True))
            p = jnp.exp2(s - m_new)
            alpha = jnp.exp2(m - m_new)
            l = alpha * l + jnp.sum(p, axis=1, keepdims=True)
            acc = alpha * acc + _dot(p.astype(BF16), v_ref[pl.ds(off, tq), hh * hd:(hh + 1) * hd])
            out.append((m_new, l, acc))
        return tuple(out)

    init = tuple((jnp.full((tq, 1), NEG_BIG, F32), jnp.zeros((tq, 1), F32), jnp.zeros((tq, hd), F32))
                 for _ in range(nh))
    carry = lax.fori_loop(0, qi, lambda j, c: step(j, c, False), init)
    carry = step(qi, carry, True)
    for hh in range(nh):
        _, l, acc = carry[hh]
        o_ref[:, hh * hd:(hh + 1) * hd] = (acc / l).astype(o_ref.dtype)


FOX_HEADS_PER_STEP = 4


def fox_attention(pmain, c, batch, seq, tq=512):
    n = batch * seq
    tq = min(tq, seq)
    nq = seq // tq
    nh = FOX_HEADS_PER_STEP
    wb = nh * FOX_HEAD_DIM
    per_d = D_MODEL // wb
    return pl.pallas_call(
        functools.partial(_fox_kernel, tq=tq, seq=seq, nh=nh),
        out_shape=jax.ShapeDtypeStruct((n, D_MODEL), BF16),
        grid=(batch, H_FOX // nh, nq),
        in_specs=[
            pl.BlockSpec((tq, wb), lambda b, h, i: (b * nq + i, COL_Q * per_d + h)),
            pl.BlockSpec((seq, wb), lambda b, h, i: (b, COL_K * per_d + h)),
            pl.BlockSpec((seq, wb), lambda b, h, i: (b, COL_V * per_d + h)),
            pl.BlockSpec((tq, LANES), lambda b, h, i: (b * nq + i, 0)),
            pl.BlockSpec((seq, LANES), lambda b, h, i: (b, 0)),
        ],
        out_specs=pl.BlockSpec((tq, wb), lambda b, h, i: (b * nq + i, h)),
        scratch_shapes=[pltpu.VMEM((nh, seq, FOX_HEAD_DIM + FOX_AUG), BF16)],
        compiler_params=_cparams(("parallel", "parallel", "arbitrary")),
    )(pmain, pmain, pmain, c, c)


def _merge_kernel(y_ref, o_ref, wbs_ref, wbf_ref, gs_ref, gf_ref, u_ref):
    a = _dot(y_ref[...], wbs_ref[...])
    b = _dot(o_ref[...], wbf_ref[...])
    u = _sigmoid(gs_ref[...].astype(F32)) * a + _sigmoid(gf_ref[...].astype(F32)) * b
    u_ref[...] = u.astype(u_ref.dtype)


def branch_merge(y_ssd, o_fox, wbs, wbf, pmain, tm=1024, tn=512):
    n = y_ssd.shape[0]
    tm = min(tm, n)
    per_d = D_MODEL // tn
    return pl.pallas_call(
        _merge_kernel,
        out_shape=jax.ShapeDtypeStruct((n, D_MODEL), BF16),
        grid=(n // tm, D_MODEL // tn),
        in_specs=[
            pl.BlockSpec((tm, D_MODEL), lambda i, j: (i, 0)),
            pl.BlockSpec((tm, D_MODEL), lambda i, j: (i, 0)),
            pl.BlockSpec((D_MODEL, tn), lambda i, j: (0, j)),
            pl.BlockSpec((D_MODEL, tn), lambda i, j: (0, j)),
            pl.BlockSpec((tm, tn), lambda i, j: (i, COL_GSSD * per_d + j)),
            pl.BlockSpec((tm, tn), lambda i, j: (i, COL_GFOX * per_d + j)),
        ],
        out_specs=pl.BlockSpec((tm, tn), lambda i, j: (i, j)),
        compiler_params=_cparams(("parallel", "arbitrary")),
    )(y_ssd, o_fox, wbs, wbf, pmain, pmain)


def _outproj_kernel(u_ref, w_ref, x_ref, o_ref):
    o_ref[...] = x_ref[...] + _dot(u_ref[...], w_ref[...])


def out_proj_residual(u, w_out, x, tm=1024, tn=1024):
    n = u.shape[0]
    tm = min(tm, n)
    return pl.pallas_call(
        _outproj_kernel,
        out_shape=jax.ShapeDtypeStruct((n, D_MODEL), F32),
        grid=(n // tm, D_MODEL // tn),
        in_specs=[
            pl.BlockSpec((tm, D_MODEL), lambda i, j: (i, 0)),
            pl.BlockSpec((D_MODEL, tn), lambda i, j: (0, j)),
            pl.BlockSpec((tm, tn), lambda i, j: (i, j)),
        ],
        out_specs=pl.BlockSpec((tm, tn), lambda i, j: (i, j)),
        compiler_params=_cparams(("parallel", "arbitrary")),
    )(u, w_out, x)


def _router_kernel(x_ref, nw_ref, rw_ref, rb_ref, h_ref, idx_ref, rank_ref, gate_ref, cnt_ref, carry):
    tm = x_ref.shape[0]

    @pl.when(pl.program_id(0) == 0)
    def _():
        carry[...] = jnp.zeros(carry.shape, F32)

    x = x_ref[...]
    h = x * lax.rsqrt(jnp.mean(x * x, axis=-1, keepdims=True) + EPS) * nw_ref[...]
    h_ref[...] = h
    work = _dot(h.astype(BF16), rw_ref[...]) + rb_ref[...]
    lane = lax.broadcasted_iota(jnp.int32, work.shape, 1)
    tops, idxs = [], []
    for _ in range(TOP_K):
        mx = jnp.max(work, axis=1, keepdims=True)
        ix = jnp.min(jnp.where(work == mx, lane, LANES), axis=1, keepdims=True)
        tops.append(mx)
        idxs.append(ix)
        work = jnp.where(lane == ix, -jnp.inf, work)
    es = [jnp.exp(t - tops[0]) for t in tops]
    den = es[0] + es[1] + es[2] + es[3]

    chosen = jnp.zeros(work.shape, F32)
    for k in range(TOP_K):
        chosen = jnp.where(lane == idxs[k], 1.0, chosen)
    row = lax.broadcasted_iota(jnp.int32, (tm, tm), 0)
    col = lax.broadcasted_iota(jnp.int32, (tm, tm), 1)
    before = _dot(jnp.where(col < row, 1.0, 0.0).astype(BF16), chosen.astype(BF16)) + carry[...]
    total = before[tm - 1:tm, :] + chosen[tm - 1:tm, :]
    carry[...] = total
    cnt_ref[...] = total.astype(jnp.int32)

    gates = jnp.zeros(work.shape, F32)
    idx = jnp.zeros(work.shape, jnp.int32)
    rank = jnp.zeros(work.shape, jnp.int32)
    for k in range(TOP_K):
        rk = jnp.sum(jnp.where(lane == idxs[k], before, 0.0), axis=1, keepdims=True).astype(jnp.int32)
        gates = jnp.where(lane == k, es[k] / den, gates)
        idx = jnp.where(lane == k, idxs[k], idx)
        rank = jnp.where(lane == k, rk, rank)
    gate_ref[...] = gates
    idx_ref[...] = idx
    rank_ref[...] = rank


def ffn_norm_router(x, norm_w, rw_pad, rb_pad, tm=512):
    n = x.shape[0]
    tm = min(tm, n)
    tile = lambda i: (i, 0)
    const = lambda i: (0, 0)
    return pl.pallas_call(
        _router_kernel,
        out_shape=(jax.ShapeDtypeStruct((n, D_MODEL), F32),
                   jax.ShapeDtypeStruct((n, LANES), jnp.int32),
                   jax.ShapeDtypeStruct((n, LANES), jnp.int32),
                   jax.ShapeDtypeStruct((n, LANES), F32),
                   jax.ShapeDtypeStruct((1, LANES), jnp.int32)),
        grid=(n // tm,),
        in_specs=[
            pl.BlockSpec((tm, D_MODEL), tile),
            pl.BlockSpec((1, D_MODEL), const),
            pl.BlockSpec((D_MODEL, LANES), const),
            pl.BlockSpec((1, LANES), const),
        ],
        out_specs=(pl.BlockSpec((tm, D_MODEL), tile),
                   pl.BlockSpec((tm, LANES), tile),
                   pl.BlockSpec((tm, LANES), tile),
                   pl.BlockSpec((tm, LANES), tile),
                   pl.BlockSpec((1, LANES), const)),
        scratch_shapes=[pltpu.VMEM((1, LANES), F32)],
        compiler_params=_cparams(("arbitrary",)),
    )(x, norm_w.reshape(1, D_MODEL), rw_pad, rb_pad)


PAIRS_PER_IDX_ROW = LANES // TOP_K


def _dispatch_kernel(tab_ref, dest_ref, h_ref, xs_hbm, zrow, sem, zsem, *, tm):
    i = pl.program_id(0)

    def issue(rr, c):
        for cc in range(LANES):
            r = rr * PAIRS_PER_IDX_ROW + cc // TOP_K
            pltpu.make_async_copy(h_ref.at[pl.ds(r, 1)], xs_hbm.at[pl.ds(dest_ref[rr, cc], 1)], sem).start(
                priority=cc % 2)
        return c

    lax.fori_loop(0, tm // PAIRS_PER_IDX_ROW, issue, 0)

    @pl.when(i == 0)
    def _():
        zrow[...] = jnp.zeros(zrow.shape, F32)

        def per_expert(e, c):
            first = tab_ref[0, e] + tab_ref[1, e]
            n_pad = tab_ref[2, e]

            def zero_row(r, c2):
                pltpu.make_async_copy(zrow.at[pl.ds(0, 1)], xs_hbm.at[pl.ds(first + r, 1)], zsem).start()
                return c2

            def wait_row(r, c2):
                pltpu.make_async_copy(zrow.at[pl.ds(0, 1)], xs_hbm.at[pl.ds(first + r, 1)], zsem).wait()
                return c2

            lax.fori_loop(0, n_pad, zero_row, 0)
            lax.fori_loop(0, n_pad, wait_row, 0)
            return c

        lax.fori_loop(0, N_EXPERTS, per_expert, 0)

        used = tab_ref[3, 0]
        n_tail = lax.shift_right_logical(xs_hbm.shape[0] - used, 3)

        def tail_copy(j):
            return pltpu.make_async_copy(zrow, xs_hbm.at[pl.ds(pl.multiple_of(used + j * 8, 8), 8)], zsem)

        lax.fori_loop(0, n_tail, lambda j, c: (tail_copy(j).start(), c)[1], 0)
        lax.fori_loop(0, n_tail, lambda j, c: (tail_copy(j).wait(), c)[1], 0)

    pltpu.make_async_copy(xs_hbm.at[pl.ds(0, TOP_K * tm)], xs_hbm.at[pl.ds(0, TOP_K * tm)], sem).wait()


def moe_dispatch(h2, dest2d, tab, n_slots, tm=512):
    n = h2.shape[0]
    tm = min(tm, n)
    return pl.pallas_call(
        functools.partial(_dispatch_kernel, tm=tm),
        out_shape=jax.ShapeDtypeStruct((n_slots, D_MODEL), F32),
        grid_spec=pltpu.PrefetchScalarGridSpec(
            num_scalar_prefetch=1,
            grid=(n // tm,),
            in_specs=[
                pl.BlockSpec((tm // PAIRS_PER_IDX_ROW, LANES), lambda i, tab: (i, 0), memory_space=pltpu.SMEM),
                pl.BlockSpec((tm, D_MODEL), lambda i, tab: (i, 0)),
            ],
            out_specs=pl.BlockSpec(memory_space=pl.ANY),
            scratch_shapes=[pltpu.VMEM((8, D_MODEL), F32), pltpu.SemaphoreType.DMA(()), pltpu.SemaphoreType.DMA(())],
        ),
        compiler_params=_cparams(("arbitrary",)),
    )(tab, dest2d, h2)


def _moe_kernel(be_ref, nu_ref, xs_ref, wg_ref, bg_ref, wu_ref, bu_ref, wd_ref, bd_ref, y_ref):
    i = pl.program_id(0)

    @pl.when(i < nu_ref[0])
    def _():
        xb = xs_ref[...].astype(BF16)
        a = jnp.minimum(_dot(xb, wg_ref[0]) + bg_ref[0], SWIGLU_LIMIT)
        up = jnp.clip(_dot(xb, wu_ref[0]) + bu_ref[0], -SWIGLU_LIMIT, SWIGLU_LIMIT)
        act = (up + 1.0) * (a * _sigmoid(SWIGLU_ALPHA * a))
        y_ref[...] = _dot(act.astype(BF16), wd_ref[0]) + bd_ref[0]

    @pl.when(i >= nu_ref[0])
    def _():
        y_ref[...] = jnp.zeros(y_ref.shape, y_ref.dtype)


def moe_experts(xs, block_expert, n_used, wg, bg, wu, bu, wd, bd):
    n_slots = xs.shape[0]
    tb = MOE_BLOCK
    emap = lambda i, be, nu: (be[i], 0, 0)
    xmap = lambda i, be, nu: (jnp.minimum(i, nu[0] - 1), 0)
    return pl.pallas_call(
        _moe_kernel,
        out_shape=jax.ShapeDtypeStruct((n_slots, D_MODEL), F32),
        grid_spec=pltpu.PrefetchScalarGridSpec(
            num_scalar_prefetch=2,
            grid=(n_slots // tb,),
            in_specs=[
                pl.BlockSpec((tb, D_MODEL), xmap),
                pl.BlockSpec((1, D_MODEL, D_FF), emap),
                pl.BlockSpec((1, 1, D_FF), emap),
                pl.BlockSpec((1, D_MODEL, D_FF), emap),
                pl.BlockSpec((1, 1, D_FF), emap),
                pl.BlockSpec((1, D_FF, D_MODEL), emap),
                pl.BlockSpec((1, 1, D_MODEL), emap),
            ],
            out_specs=pl.BlockSpec((tb, D_MODEL), lambda i, be, nu: (i, 0)),
        ),
        compiler_params=_cparams(("arbitrary",)),
    )(block_expert, n_used, xs, wg, bg, wu, bu, wd, bd)


def _combine_kernel(dcur_ref, dnext_ref, x_ref, g_ref, nw_ref, ys_hbm, o_ref, buf, sems, *, tm, final_norm):
    i = pl.program_id(0)
    n_steps = pl.num_programs(0)
    slot = i % 2

    def issue(dest_ref, s):
        def body(rr, c):
            for cc in range(LANES):
                r = rr * PAIRS_PER_IDX_ROW + cc // TOP_K
                pltpu.make_async_copy(ys_hbm.at[pl.ds(dest_ref[rr, cc], 1)],
                                      buf.at[s, cc % TOP_K, pl.ds(r, 1)], sems.at[s]).start(priority=cc % 2)
            return c
        lax.fori_loop(0, tm // PAIRS_PER_IDX_ROW, body, 0)

    @pl.when(i == 0)
    def _():
        issue(dcur_ref, 0)

    @pl.when(i + 1 < n_steps)
    def _():
        issue(dnext_ref, 1 - slot)

    for k in range(TOP_K):
        pltpu.make_async_copy(ys_hbm.at[pl.ds(0, tm)], buf.at[slot, k], sems.at[slot]).wait()
    g = g_ref[...]
    acc = x_ref[...]
    for k in range(TOP_K):
        acc = acc + g[:, k:k + 1] * buf[slot, k]
    if final_norm:
        acc = acc * lax.rsqrt(jnp.mean(acc * acc, axis=-1, keepdims=True) + EPS) * nw_ref[...]
    o_ref[...] = acc


def moe_combine(x, ys, dest2d, gates, norm_w, final_norm, tm=256):
    n = x.shape[0]
    tm = min(tm, n)
    n_steps = n // tm
    tile = lambda i: (i, 0)
    return pl.pallas_call(
        functools.partial(_combine_kernel, tm=tm, final_norm=final_norm),
        out_shape=jax.ShapeDtypeStruct((n, D_MODEL), F32),
        grid=(n_steps,),
        in_specs=[
            pl.BlockSpec((tm // PAIRS_PER_IDX_ROW, LANES), tile, memory_space=pltpu.SMEM),
            pl.BlockSpec((tm // PAIRS_PER_IDX_ROW, LANES), lambda i: (jnp.minimum(i + 1, n_steps - 1), 0),
                         memory_space=pltpu.SMEM),
            pl.BlockSpec((tm, D_MODEL), tile),
            pl.BlockSpec((tm, LANES), tile),
            pl.BlockSpec((1, D_MODEL), lambda i: (0, 0)),
            pl.BlockSpec(memory_space=pl.ANY),
        ],
        out_specs=pl.BlockSpec((tm, D_MODEL), tile),
        scratch_shapes=[pltpu.VMEM((2, TOP_K, tm, D_MODEL), F32), pltpu.SemaphoreType.DMA((2,))],
        compiler_params=_cparams(("arbitrary",)),
    )(dest2d, dest2d, x, gates, norm_w.reshape(1, D_MODEL), ys)


def _slot_tables(idx, rank, counts, n_tok):
    tb = MOE_BLOCK
    n_pairs = n_tok * TOP_K
    n_blocks = (n_pairs + N_EXPERTS * (tb - 1) + tb - 1) // tb
    padded = (counts + tb - 1) // tb * tb
    padded_end = jnp.cumsum(padded)
    first = padded_end - padded
    experts = jnp.arange(N_EXPERTS, dtype=jnp.int32)
    dest = jnp.sum(jnp.where(idx[:, :, None] == experts, first, 0), axis=-1) + rank
    dest2d = dest.reshape(n_tok // PAIRS_PER_IDX_ROW, LANES)
    block_start = jnp.arange(n_blocks, dtype=jnp.int32) * tb
    block_expert = jnp.minimum(jnp.sum(padded_end[None, :] <= block_start[:, None], axis=1),
                               N_EXPERTS - 1).astype(jnp.int32)
    n_used = (padded_end[-1] // tb).astype(jnp.int32).reshape(1)
    tab = jnp.stack([first, counts, padded - counts, jnp.broadcast_to(padded_end[-1], (N_EXPERTS,))]).astype(jnp.int32)
    return dest2d, tab, block_expert, n_used, n_blocks * tb


def _pack_in_proj(w_in):
    z, xbc, dtw, qkv, fw, g = jnp.split(
        w_in, [D_MODEL, D_MODEL + CONV_DIM, D_MODEL + CONV_DIM + H_SSD,
               D_MODEL + CONV_DIM + H_SSD + 3 * D_MODEL, D_MODEL + CONV_DIM + H_SSD + 3 * D_MODEL + H_FOX], axis=1)
    g_ssd, g_fox = jnp.split(g, 2, axis=1)
    q_w, kv_w = qkv[:, :D_MODEL] * FOX_QSCALE, qkv[:, D_MODEL:]
    w_main = jnp.concatenate([xbc, z, g_ssd, q_w, kv_w, g_fox], axis=1).astype(BF16)
    pad = jnp.zeros((D_MODEL, LANES - H_FOX - H_SSD), w_in.dtype)
    w_small = jnp.concatenate([fw, dtw, pad], axis=1).astype(BF16)
    return w_main, w_small


def _pad_lanes(f_part, dt_part):
    return jnp.concatenate([f_part, dt_part, jnp.zeros((LANES - H_FOX - H_SSD,), F32)]).reshape(1, LANES)


def kernel(x, norm_mix_w, w_in, conv_w, conv_b, dt_bias, a_log, d_skip, ssd_norm_w, b_forget, w_branch_ssd,
           w_branch_fox, w_out, norm_ffn_w, router_w, router_b, w_gate, b_gate, w_up, b_up, w_down, b_down,
           norm_final_w):
    batch, seq, d = x.shape
    n = batch * seq
    depth = w_in.shape[0]
    x = x.reshape(n, d)
    for l in range(depth):
        w_main, w_small = _pack_in_proj(w_in[l])
        pmain, psmall = norm_in_proj(x, norm_mix_w[l], w_main, w_small)
        sm_bias = _pad_lanes(b_forget[l], dt_bias[l])
        alog_row = _pad_lanes(jnp.zeros((H_FOX,), F32), a_log[l])
        dskip_row = jnp.repeat(d_skip[l], SSD_HEAD_DIM).reshape(1, D_MODEL)
        y_ssd, c = ssd_branch(pmain, psmall, conv_w[l], conv_b[l].reshape(1, CONV_DIM), sm_bias, alog_row,
                              dskip_row, ssd_norm_w[l].reshape(1, D_MODEL), batch, seq)
        o_fox = fox_attention(pmain, c, batch, seq)
        u = branch_merge(y_ssd, o_fox, w_branch_ssd[l].astype(BF16), w_branch_fox[l].astype(BF16), pmain)
        x = out_proj_residual(u, w_out[l].astype(BF16), x)
        rw_pad = jnp.concatenate([router_w[l], jnp.zeros((D_MODEL, LANES - N_EXPERTS), F32)], axis=1).astype(BF16)
        rb_pad = jnp.concatenate([router_b[l], jnp.full((LANES - N_EXPERTS,), NEG_BIG, F32)]).reshape(1, LANES)
        h2, idx_pad, rank_pad, gates_pad, counts_pad = ffn_norm_router(x, norm_ffn_w[l], rw_pad, rb_pad)
        dest2d, tab, block_expert, n_used, n_slots = _slot_tables(
            idx_pad[:, :TOP_K], rank_pad[:, :TOP_K], counts_pad[0, :N_EXPERTS], n)
        xs = moe_dispatch(h2, dest2d, tab, n_slots)
        ys = moe_experts(xs, block_expert, n_used,
                         w_gate[l].astype(BF16), b_gate[l].reshape(N_EXPERTS, 1, D_FF),
                         w_up[l].astype(BF16), b_up[l].reshape(N_EXPERTS, 1, D_FF),
                         w_down[l].astype(BF16), b_down[l].reshape(N_EXPERTS, 1, D_MODEL))
        x = moe_combine(x, ys, dest2d, gates_pad, norm_final_w, final_norm=(l == depth - 1))
    return x.reshape(batch, seq, d)
```

```python
import functools
import math

import jax
import jax.numpy as jnp
from jax import lax
from jax.experimental import pallas as pl
from jax.experimental.pallas import tpu as pltpu

F32 = jnp.float32
BF16 = jnp.bfloat16

D_MODEL = 2048
EPS = 1e-5
SSD_HEAD_DIM = 64
H_SSD = 32
SSD_GROUPS = 8
SSD_STATE = 128
CONV_K = 4
CONV_DIM = D_MODEL + 2 * SSD_GROUPS * SSD_STATE
SSD_CHUNK = 128
FOX_HEAD_DIM = 128
H_FOX = 16
N_EXPERTS = 32
TOP_K = 4
D_FF = 768
SWIGLU_LIMIT = 7.0
SWIGLU_ALPHA = 1.702

LANES = 128
VMEM_LIMIT = 56 * 1024 * 1024
NEG_BIG = -1e30

COL_XBC, COL_Z, COL_GSSD, COL_Q, COL_K, COL_V, COL_GFOX = 0, 2, 3, 4, 5, 6, 7
SM_DT0 = H_FOX

MOE_BLOCK = 512


def _cparams(sem):
    return pltpu.CompilerParams(dimension_semantics=sem, vmem_limit_bytes=VMEM_LIMIT)


def _dot(a, b):
    return jnp.dot(a, b, preferred_element_type=F32)


def _dot_nt(a, b):
    return lax.dot_general(a, b, (((1,), (1,)), ((), ())), preferred_element_type=F32)


def _sigmoid(x):
    return 0.5 * jnp.tanh(0.5 * x) + 0.5


def _norm_proj_kernel(x_ref, nw_ref, w_ref, ws_ref, o_ref, os_ref, h_scr):
    @pl.when(pl.program_id(1) == 0)
    def _():
        rb = min(256, x_ref.shape[0])
        for r in range(x_ref.shape[0] // rb):
            rows = slice(r * rb, (r + 1) * rb)
            x = x_ref[rows, :]
            h = (x * lax.rsqrt(jnp.mean(x * x, axis=-1, keepdims=True) + EPS) * nw_ref[...]).astype(BF16)
            h_scr[rows, :] = h
            os_ref[rows, :] = _dot(h, ws_ref[...])

    o_ref[...] = _dot(h_scr[...], w_ref[...]).astype(o_ref.dtype)


def norm_in_proj(x, norm_w, w_main, w_small, tm=1024, tn=1024):
    n, d = x.shape
    n_main, n_small = w_main.shape[1], w_small.shape[1]
    tm = min(tm, n)
    return pl.pallas_call(
        _norm_proj_kernel,
        out_shape=(jax.ShapeDtypeStruct((n, n_main), BF16), jax.ShapeDtypeStruct((n, n_small), F32)),
        grid=(n // tm, n_main // tn),
        in_specs=[
            pl.BlockSpec((tm, d), lambda i, j: (i, 0)),
            pl.BlockSpec((1, d), lambda i, j: (0, 0)),
            pl.BlockSpec((d, tn), lambda i, j: (0, j)),
            pl.BlockSpec((d, n_small), lambda i, j: (0, 0)),
        ],
        out_specs=(pl.BlockSpec((tm, tn), lambda i, j: (i, j)),
                   pl.BlockSpec((tm, n_small), lambda i, j: (i, 0))),
        scratch_shapes=[pltpu.VMEM((tm, d), BF16)],
        compiler_params=_cparams(("parallel", "arbitrary")),
    )(x, norm_w.reshape(1, d), w_main, w_small)


def _cumsum_rows(val):
    L = val.shape[0]
    row = lax.broadcasted_iota(jnp.int32, (L, L), 0)
    col = lax.broadcasted_iota(jnp.int32, (L, L), 1)
    tri = jnp.where(row >= col, 1.0, 0.0).astype(BF16)
    hi = val.astype(BF16)
    r1 = val - hi.astype(F32)
    mid = r1.astype(BF16)
    lo = (r1 - mid.astype(F32)).astype(BF16)
    return _dot(tri, hi) + _dot(tri, mid) + _dot(tri, lo)


def _ssd_kernel(xbc_ref, z_ref, sm_ref, convw_ref, convb_ref, bias_ref, alog_ref, dskip_ref, normw_ref,
                y_ref, c_ref, ubuf, xc, ybuf, state, carry):
    L = SSD_CHUNK
    ci = pl.program_id(1)

    @pl.when(ci == 0)
    def _():
        ubuf[0:8, :] = jnp.zeros((8, CONV_DIM), F32)
        state[...] = jnp.zeros(state.shape, F32)
        carry[...] = jnp.zeros(carry.shape, F32)

    ubuf[8:8 + L, :] = xbc_ref[...].astype(F32)
    ct_w = 512
    for t in range(CONV_DIM // ct_w):
        sl = slice(t * ct_w, (t + 1) * ct_w)
        u = ubuf[:, sl]
        u1 = pltpu.roll(u, 1, axis=0)
        near = convw_ref[3:4, sl] * u + convw_ref[2:3, sl] * u1
        far = convw_ref[1:2, sl] * u + convw_ref[0:1, sl] * u1
        acc = (near + pltpu.roll(far, 2, axis=0))[8:, :] + convb_ref[:, sl]
        xc[:, sl] = acc * _sigmoid(acc)
    ubuf[0:8, :] = ubuf[L:L + 8, :]

    lane = lax.broadcasted_iota(jnp.int32, (L, LANES), 1)
    lane1 = lax.broadcasted_iota(jnp.int32, (1, LANES), 1)
    v = sm_ref[...] + bias_ref[...]
    sp = jnp.log1p(jnp.exp(-jnp.abs(v)))
    logf = jnp.minimum(v, 0.0) - sp
    dt = jnp.maximum(v, 0.0) + sp
    is_dt1 = (lane1 >= SM_DT0) & (lane1 < SM_DT0 + H_SSD)
    a_row = jnp.where(is_dt1, -jnp.exp(alog_ref[...]), 0.0)
    cs = _cumsum_rows(jnp.where(lane < SM_DT0, logf, dt * a_row))
    c_full = cs + carry[...]
    c_ref[...] = c_full
    carry[...] = c_full[L - 1:L, :]

    cs_t = cs.T
    last = cs[L - 1:L, :]
    e_in = jnp.exp(cs)
    e_out = jnp.exp(last - cs)
    e_all = jnp.exp(last)
    row = lax.broadcasted_iota(jnp.int32, (L, L), 0)
    col = lax.broadcasted_iota(jnp.int32, (L, L), 1)
    causal = row >= col
    lo_half = lane < SSD_HEAD_DIM
    lo_half1 = lane1 < SSD_HEAD_DIM

    for g in range(SSD_GROUPS):
        bm = xc[:, D_MODEL + g * SSD_STATE:D_MODEL + (g + 1) * SSD_STATE]
        cm = xc[:, D_MODEL + (SSD_GROUPS + g) * SSD_STATE:D_MODEL + (SSD_GROUPS + g + 1) * SSD_STATE]
        bmb = bm.astype(BF16)
        cmb = cm.astype(BF16)
        cb = _dot_nt(cmb, bmb)
        bm_t = bm.T.astype(BF16)
        for pr in range(2):
            h0 = g * 4 + pr * 2
            k0, k1 = SM_DT0 + h0, SM_DT0 + h0 + 1
            pidx = h0 // 2

            def pair(qarr, k0=k0, k1=k1):
                return jnp.where(lo_half, qarr[:, k0:k0 + 1], qarr[:, k1:k1 + 1])

            xs_p = xc[:, h0 * SSD_HEAD_DIM:h0 * SSD_HEAD_DIM + LANES]
            xd = xs_p * pair(dt)
            xdb = xd.astype(BF16)
            ys = []
            for k in (k0, k1):
                seg = cs[:, k:k + 1] - cs_t[k:k + 1, :]
                lm = jnp.exp(jnp.where(causal, seg, NEG_BIG))
                ys.append(_dot((cb * lm).astype(BF16), xdb))
            y_diag = jnp.where(lo_half, ys[0], ys[1])
            st_prev = state[pidx]
            y_off = _dot(cmb, st_prev.astype(BF16)) * pair(e_in)
            st_new = _dot(bm_t, (xd * pair(e_out)).astype(BF16))
            e_pair = jnp.where(lo_half1, e_all[:, k0:k0 + 1], e_all[:, k1:k1 + 1])
            state[pidx] = e_pair * st_prev + st_new
            psl = slice(h0 * SSD_HEAD_DIM, h0 * SSD_HEAD_DIM + LANES)
            ybuf[:, psl] = y_diag + y_off + dskip_ref[:, psl] * xs_p

    gw = D_MODEL // SSD_GROUPS
    for g in range(SSD_GROUPS):
        sl = slice(g * gw, (g + 1) * gw)
        zz = z_ref[:, sl].astype(F32)
        gz = ybuf[:, sl] * (zz * _sigmoid(zz))
        gz = gz * lax.rsqrt(jnp.mean(gz * gz, axis=-1, keepdims=True) + EPS)
        y_ref[:, sl] = (gz * normw_ref[:, sl]).astype(y_ref.dtype)


def ssd_branch(pmain, psmall, conv_w, conv_b, sm_bias, alog_row, dskip_row, norm_w, batch, seq):
    n = batch * seq
    L = SSD_CHUNK
    nc = seq // L
    rowmap = lambda b, c: (b * nc + c, 0)
    const = lambda b, c: (0, 0)
    return pl.pallas_call(
        _ssd_kernel,
        out_shape=(jax.ShapeDtypeStruct((n, D_MODEL), BF16),
                   jax.ShapeDtypeStruct((n, LANES), F32)),
        grid=(batch, nc),
        in_specs=[
            pl.BlockSpec((L, CONV_DIM), lambda b, c: (b * nc + c, COL_XBC)),
            pl.BlockSpec((L, D_MODEL), lambda b, c: (b * nc + c, COL_Z)),
            pl.BlockSpec((L, LANES), rowmap),
            pl.BlockSpec((CONV_K, CONV_DIM), const),
            pl.BlockSpec((1, CONV_DIM), const),
            pl.BlockSpec((1, LANES), const),
            pl.BlockSpec((1, LANES), const),
            pl.BlockSpec((1, D_MODEL), const),
            pl.BlockSpec((1, D_MODEL), const),
        ],
        out_specs=(pl.BlockSpec((L, D_MODEL), rowmap),
                   pl.BlockSpec((L, LANES), rowmap)),
        scratch_shapes=[
            pltpu.VMEM((L + 8, CONV_DIM), F32),
            pltpu.VMEM((L, CONV_DIM), F32),
            pltpu.VMEM((L, D_MODEL), F32),
            pltpu.VMEM((H_SSD // 2, SSD_STATE, LANES), F32),
            pltpu.VMEM((1, LANES), F32),
        ],
        compiler_params=_cparams(("parallel", "arbitrary")),
    )(pmain, pmain, psmall, conv_w, conv_b, sm_bias, alog_row, dskip_row, norm_w)


LOG2E = math.log2(math.e)
FOX_AUG = LANES
FOX_QSCALE = LOG2E / math.sqrt(FOX_HEAD_DIM)


def _split3_lanes(vals, first):
    rows = vals.shape[0]
    hi = vals.astype(BF16).astype(F32)
    r1 = vals - hi
    mid = r1.astype(BF16).astype(F32)
    lo = (r1 - mid).astype(BF16).astype(F32)
    lane = lax.broadcasted_iota(jnp.int32, (rows, FOX_AUG), 1)
    ones = jnp.where(lane < 6, 1.0, 0.0)
    return jnp.where(lane == first, hi, jnp.where(lane == first + 1, mid, jnp.where(lane == first + 2, lo, ones)))


def _head_column(c_tile, h):
    lane = lax.broadcasted_iota(jnp.int32, c_tile.shape, 1)
    return jnp.sum(jnp.where(lane == h, c_tile, 0.0), axis=1, keepdims=True)


def _fox_kernel(q_ref, k_ref, v_ref, cq_ref, call_ref, o_ref, kaug, *, tq, seq, nh):
    hp = pl.program_id(1)
    qi = pl.program_id(2)
    rb = min(256, seq)
    hd = FOX_HEAD_DIM

    @pl.when(qi == 0)
    def _():
        for hh in range(nh):
            for r in range(seq // rb):
                rows = slice(r * rb, (r + 1) * rb)
                ck = _head_column(call_ref[rows, :], hp * nh + hh) * LOG2E
                kaug[hh, rows, 0:hd] = k_ref[rows, hh * hd:(hh + 1) * hd]
                kaug[hh, rows, hd:hd + FOX_AUG] = _split3_lanes(-ck, 3).astype(BF16)

    q_aug = []
    for hh in range(nh):
        cq = _head_column(cq_ref[...], hp * nh + hh) * LOG2E
        q_aug.append(jnp.concatenate([q_ref[:, hh * hd:(hh + 1) * hd], _split3_lanes(cq, 0).astype(BF16)], axis=1))
    row = lax.broadcasted_iota(jnp.int32, (tq, tq), 0)
    col = lax.broadcasted_iota(jnp.int32, (tq, tq), 1)

    def step(j, carry, masked):
        off = pl.multiple_of(j * tq, tq)
        heads = range(nh)
        ss = [_dot_nt(q_aug[hh], kaug[hh, pl.ds(off, tq), :]) for hh in heads]
        if masked:
            ss = [jnp.where(row >= col, s, NEG_BIG) for s in ss]
        ms = [jnp.maximum(carry[hh][0], jnp.max(ss[hh], axis=1, keepdims=True)) for hh in heads]
        ps = [jnp.exp2(ss[hh] - ms[hh]) for hh in heads]
        alphas = [jnp.exp2(carry[hh][0] - ms[hh]) for hh in heads]
        ls = [alphas[hh] * carry[hh][1] + jnp.sum(ps[hh], axis=1, keepdims=True) for hh in heads]
        accs = [alphas[hh] * carry[hh][2] + _dot(ps[hh].astype(BF16), v_ref[pl.ds(off, tq), hh * hd:(hh + 1) * hd])
                for hh in heads]
        return tuple((ms[hh], ls[hh], accs[hh]) for hh in heads)

    init = tuple((jnp.full((tq, 1), NEG_BIG, F32), jnp.zeros((tq, 1), F32), jnp.zeros((tq, hd), F32))
                 for _ in range(nh))
    carry = lax.fori_loop(0, qi, lambda j, c: step(j, c, False), init)
    carry = step(qi, carry, True)
    for hh in range(nh):
        _, l, acc = carry[hh]
        o_ref[:, hh * hd:(hh + 1) * hd] = (acc / l).astype(o_ref.dtype)


FOX_HEADS_PER_STEP = 4


def fox_attention(pmain, c, batch, seq, tq=512):
    n = batch * seq
    tq = min(tq, seq)
    nq = seq // tq
    nh = FOX_HEADS_PER_STEP
    wb = nh * FOX_HEAD_DIM
    per_d = D_MODEL // wb
    return pl.pallas_call(
        functools.partial(_fox_kernel, tq=tq, seq=seq, nh=nh),
        out_shape=jax.ShapeDtypeStruct((n, D_MODEL), BF16),
        grid=(batch, H_FOX // nh, nq),
        in_specs=[
            pl.BlockSpec((tq, wb), lambda b, h, i: (b * nq + i, COL_Q * per_d + h)),
            pl.BlockSpec((seq, wb), lambda b, h, i: (b, COL_K * per_d + h)),
            pl.BlockSpec((seq, wb), lambda b, h, i: (b, COL_V * per_d + h)),
            pl.BlockSpec((tq, LANES), lambda b, h, i: (b * nq + i, 0)),
            pl.BlockSpec((seq, LANES), lambda b, h, i: (b, 0)),
        ],
        out_specs=pl.BlockSpec((tq, wb), lambda b, h, i: (b * nq + i, h)),
        scratch_shapes=[pltpu.VMEM((nh, seq, FOX_HEAD_DIM + FOX_AUG), BF16)],
        compiler_params=_cparams(("parallel", "parallel", "arbitrary")),
    )(pmain, pmain, pmain, c, c)


def _merge_kernel(y_ref, o_ref, wbs_ref, wbf_ref, gs_ref, gf_ref, u_ref):
    a = _dot(y_ref[...], wbs_ref[...])
    b = _dot(o_ref[...], wbf_ref[...])
    u = _sigmoid(gs_ref[...].astype(F32)) * a + _sigmoid(gf_ref[...].astype(F32)) * b
    u_ref[...] = u.astype(u_ref.dtype)


def branch_merge(y_ssd, o_fox, wbs, wbf, pmain, tm=1024, tn=512):
    n = y_ssd.shape[0]
    tm = min(tm, n)
    per_d = D_MODEL // tn
    return pl.pallas_call(
        _merge_kernel,
        out_shape=jax.ShapeDtypeStruct((n, D_MODEL), BF16),
        grid=(n // tm, D_MODEL // tn),
        in_specs=[
            pl.BlockSpec((tm, D_MODEL), lambda i, j: (i, 0)),
            pl.BlockSpec((tm, D_MODEL), lambda i, j: (i, 0)),
            pl.BlockSpec((D_MODEL, tn), lambda i, j: (0, j)),
            pl.BlockSpec((D_MODEL, tn), lambda i, j: (0, j)),
            pl.BlockSpec((tm, tn), lambda i, j: (i, COL_GSSD * per_d + j)),
            pl.BlockSpec((tm, tn), lambda i, j: (i, COL_GFOX * per_d + j)),
        ],
        out_specs=pl.BlockSpec((tm, tn), lambda i, j: (i, j)),
        compiler_params=_cparams(("parallel", "arbitrary")),
    )(y_ssd, o_fox, wbs, wbf, pmain, pmain)


def _route_tile(x, nw_ref, rw_ref, rb_ref, carry):
    tm = x.shape[0]
    h = x * lax.rsqrt(jnp.mean(x * x, axis=-1, keepdims=True) + EPS) * nw_ref[...]
    work = _dot(h.astype(BF16), rw_ref[...]) + rb_ref[...]
    lane = lax.broadcasted_iota(jnp.int32, work.shape, 1)
    tops, idxs = [], []
    for _ in range(TOP_K):
        mx = jnp.max(work, axis=1, keepdims=True)
        ix = jnp.min(jnp.where(work == mx, lane, LANES), axis=1, keepdims=True)
        tops.append(mx)
        idxs.append(ix)
        work = jnp.where(lane == ix, -jnp.inf, work)
    es = [jnp.exp(t - tops[0]) for t in tops]
    den = es[0] + es[1] + es[2] + es[3]

    chosen = jnp.zeros(work.shape, F32)
    for k in range(TOP_K):
        chosen = jnp.where(lane == idxs[k], 1.0, chosen)
    row = lax.broadcasted_iota(jnp.int32, (tm, tm), 0)
    col = lax.broadcasted_iota(jnp.int32, (tm, tm), 1)
    before = _dot(jnp.where(col < row, 1.0, 0.0).astype(BF16), chosen.astype(BF16)) + carry[...]
    total = before[tm - 1:tm, :] + chosen[tm - 1:tm, :]
    carry[...] = total

    gates = jnp.zeros(work.shape, F32)
    idx = jnp.zeros(work.shape, jnp.int32)
    rank = jnp.zeros(work.shape, jnp.int32)
    for k in range(TOP_K):
        rk = jnp.sum(jnp.where(lane == idxs[k], before, 0.0), axis=1, keepdims=True).astype(jnp.int32)
        gates = jnp.where(lane == k, es[k] / den, gates)
        idx = jnp.where(lane == k, idxs[k], idx)
        rank = jnp.where(lane == k, rk, rank)
    return h, idx, rank, gates, total


def _outproj_router_kernel(u_ref, w_ref, x_ref, nw_ref, rw_ref, rb_ref,
                           xo_ref, h_ref, idx_ref, rank_ref, gate_ref, cnt_ref, xfull, carry, *, nj):
    j = pl.program_id(1)
    tn = xo_ref.shape[1]

    @pl.when((pl.program_id(0) == 0) & (j == 0))
    def _():
        carry[...] = jnp.zeros(carry.shape, F32)

    xn = x_ref[...] + _dot(u_ref[...], w_ref[...])
    xo_ref[...] = xn
    for jj in range(nj):
        @pl.when(j == jj)
        def _(jj=jj):
            xfull[:, jj * tn:(jj + 1) * tn] = xn

    @pl.when(j == nj - 1)
    def _():
        h, idx, rank, gates, total = _route_tile(xfull[...], nw_ref, rw_ref, rb_ref, carry)
        h_ref[...] = h
        idx_ref[...] = idx
        rank_ref[...] = rank
        gate_ref[...] = gates
        cnt_ref[...] = total.astype(jnp.int32)


def out_proj_route(u, w_out, x, norm_w, rw_pad, rb_pad, tm=512, tn=1024):
    n = u.shape[0]
    tm = min(tm, n)
    nj = D_MODEL // tn
    tile = lambda i, j: (i, 0)
    const = lambda i, j: (0, 0)
    return pl.pallas_call(
        functools.partial(_outproj_router_kernel, nj=nj),
        out_shape=(jax.ShapeDtypeStruct((n, D_MODEL), F32),
                   jax.ShapeDtypeStruct((n, D_MODEL), F32),
                   jax.ShapeDtypeStruct((n, LANES), jnp.int32),
                   jax.ShapeDtypeStruct((n, LANES), jnp.int32),
                   jax.ShapeDtypeStruct((n, LANES), F32),
                   jax.ShapeDtypeStruct((1, LANES), jnp.int32)),
        grid=(n // tm, nj),
        in_specs=[
            pl.BlockSpec((tm, D_MODEL), tile),
            pl.BlockSpec((D_MODEL, tn), lambda i, j: (0, j)),
            pl.BlockSpec((tm, tn), lambda i, j: (i, j)),
            pl.BlockSpec((1, D_MODEL), const),
            pl.BlockSpec((D_MODEL, LANES), const),
            pl.BlockSpec((1, LANES), const),
        ],
        out_specs=(pl.BlockSpec((tm, tn), lambda i, j: (i, j)),
                   pl.BlockSpec((tm, D_MODEL), tile),
                   pl.BlockSpec((tm, LANES), tile),
                   pl.BlockSpec((tm, LANES), tile),
                   pl.BlockSpec((tm, LANES), tile),
                   pl.BlockSpec((1, LANES), const)),
        scratch_shapes=[pltpu.VMEM((tm, D_MODEL), F32), pltpu.VMEM((1, LANES), F32)],
        compiler_params=_cparams(("arbitrary", "arbitrary")),
    )(u, w_out, x, norm_w.reshape(1, D_MODEL), rw_pad, rb_pad)


PAIRS_PER_IDX_ROW = LANES // TOP_K


def _dispatch_kernel(tab_ref, dest_ref, h_ref, xs_hbm, zrow, sem, zsem, *, tm):
    i = pl.program_id(0)

    def issue(rr, c):
        for cc in range(LANES):
            r = rr * PAIRS_PER_IDX_ROW + cc // TOP_K
            pltpu.make_async_copy(h_ref.at[pl.ds(r, 1)], xs_hbm.at[pl.ds(dest_ref[rr, cc], 1)], sem).start(
                priority=cc % 2)
        return c

    lax.fori_loop(0, tm // PAIRS_PER_IDX_ROW, issue, 0)

    @pl.when(i == 0)
    def _():
        zrow[...] = jnp.zeros(zrow.shape, F32)

        def per_expert(e, c):
            first = tab_ref[0, e] + tab_ref[1, e]
            n_pad = tab_ref[2, e]

            def zero_row(r, c2):
                pltpu.make_async_copy(zrow.at[pl.ds(0, 1)], xs_hbm.at[pl.ds(first + r, 1)], zsem).start()
                return c2

            def wait_row(r, c2):
                pltpu.make_async_copy(zrow.at[pl.ds(0, 1)], xs_hbm.at[pl.ds(first + r, 1)], zsem).wait()
                return c2

            lax.fori_loop(0, n_pad, zero_row, 0)
            lax.fori_loop(0, n_pad, wait_row, 0)
            return c

        lax.fori_loop(0, N_EXPERTS, per_expert, 0)

        used = tab_ref[3, 0]
        n_tail = lax.shift_right_logical(xs_hbm.shape[0] - used, 3)

        def tail_copy(j):
            return pltpu.make_async_copy(zrow, xs_hbm.at[pl.ds(pl.multiple_of(used + j * 8, 8), 8)], zsem)

        lax.fori_loop(0, n_tail, lambda j, c: (tail_copy(j).start(), c)[1], 0)
        lax.fori_loop(0, n_tail, lambda j, c: (tail_copy(j).wait(), c)[1], 0)

    pltpu.make_async_copy(xs_hbm.at[pl.ds(0, TOP_K * tm)], xs_hbm.at[pl.ds(0, TOP_K * tm)], sem).wait()


def moe_dispatch(h2, dest2d, tab, n_slots, tm=512):
    n = h2.shape[0]
    tm = min(tm, n)
    return pl.pallas_call(
        functools.partial(_dispatch_kernel, tm=tm),
        out_shape=jax.ShapeDtypeStruct((n_slots, D_MODEL), F32),
        grid_spec=pltpu.PrefetchScalarGridSpec(
            num_scalar_prefetch=1,
            grid=(n // tm,),
            in_specs=[
                pl.BlockSpec((tm // PAIRS_PER_IDX_ROW, LANES), lambda i, tab: (i, 0), memory_space=pltpu.SMEM),
                pl.BlockSpec((tm, D_MODEL), lambda i, tab: (i, 0)),
            ],
            out_specs=pl.BlockSpec(memory_space=pl.ANY),
            scratch_shapes=[pltpu.VMEM((8, D_MODEL), F32), pltpu.SemaphoreType.DMA(()), pltpu.SemaphoreType.DMA(())],
        ),
        compiler_params=_cparams(("arbitrary",)),
    )(tab, dest2d, h2)


def _moe_kernel(be_ref, nu_ref, xs_ref, wg_ref, bg_ref, wu_ref, bu_ref, wd_ref, bd_ref, y_ref):
    i = pl.program_id(0)

    @pl.when(i < nu_ref[0])
    def _():
        xb = xs_ref[...].astype(BF16)
        a = jnp.minimum(_dot(xb, wg_ref[0]) + bg_ref[0], SWIGLU_LIMIT)
        up = jnp.clip(_dot(xb, wu_ref[0]) + bu_ref[0], -SWIGLU_LIMIT, SWIGLU_LIMIT)
        act = (up + 1.0) * (a * _sigmoid(SWIGLU_ALPHA * a))
        y_ref[...] = _dot(act.astype(BF16), wd_ref[0]) + bd_ref[0]

    @pl.when(i >= nu_ref[0])
    def _():
        y_ref[...] = jnp.zeros(y_ref.shape, y_ref.dtype)


def moe_experts(xs, block_expert, n_used, wg, bg, wu, bu, wd, bd):
    n_slots = xs.shape[0]
    tb = MOE_BLOCK
    emap = lambda i, be, nu: (be[i], 0, 0)
    xmap = lambda i, be, nu: (jnp.minimum(i, nu[0] - 1), 0)
    return pl.pallas_call(
        _moe_kernel,
        out_shape=jax.ShapeDtypeStruct((n_slots, D_MODEL), F32),
        grid_spec=pltpu.PrefetchScalarGridSpec(
            num_scalar_prefetch=2,
            grid=(n_slots // tb,),
            in_specs=[
                pl.BlockSpec((tb, D_MODEL), xmap),
                pl.BlockSpec((1, D_MODEL, D_FF), emap),
                pl.BlockSpec((1, 1, D_FF), emap),
                pl.BlockSpec((1, D_MODEL, D_FF), emap),
                pl.BlockSpec((1, 1, D_FF), emap),
                pl.BlockSpec((1, D_FF, D_MODEL), emap),
                pl.BlockSpec((1, 1, D_MODEL), emap),
            ],
            out_specs=pl.BlockSpec((tb, D_MODEL), lambda i, be, nu: (i, 0)),
        ),
        compiler_params=_cparams(("arbitrary",)),
    )(block_expert, n_used, xs, wg, bg, wu, bu, wd, bd)


def _combine_kernel(dcur_ref, dnext_ref, x_ref, g_ref, nw_ref, ys_hbm, o_ref, buf, sems, *, tm, final_norm):
    i = pl.program_id(0)
    n_steps = pl.num_programs(0)
    slot = i % 2

    def issue(dest_ref, s):
        def body(rr, c):
            for cc in range(LANES):
                r = rr * PAIRS_PER_IDX_ROW + cc // TOP_K
                pltpu.make_async_copy(ys_hbm.at[pl.ds(dest_ref[rr, cc], 1)],
                                      buf.at[s, cc % TOP_K, pl.ds(r, 1)], sems.at[s]).start(priority=cc % 2)
            return c
        lax.fori_loop(0, tm // PAIRS_PER_IDX_ROW, body, 0)

    @pl.when(i == 0)
    def _():
        issue(dcur_ref, 0)

    @pl.when(i + 1 < n_steps)
    def _():
        issue(dnext_ref, 1 - slot)

    for k in range(TOP_K):
        pltpu.make_async_copy(ys_hbm.at[pl.ds(0, tm)], buf.at[slot, k], sems.at[slot]).wait()
    g = g_ref[...]
    acc = x_ref[...]
    for k in range(TOP_K):
        acc = acc + g[:, k:k + 1] * buf[slot, k]
    if final_norm:
        acc = acc * lax.rsqrt(jnp.mean(acc * acc, axis=-1, keepdims=True) + EPS) * nw_ref[...]
    o_ref[...] = acc


def moe_combine(x, ys, dest2d, gates, norm_w, final_norm, tm=256):
    n = x.shape[0]
    tm = min(tm, n)
    n_steps = n // tm
    tile = lambda i: (i, 0)
    return pl.pallas_call(
        functools.partial(_combine_kernel, tm=tm, final_norm=final_norm),
        out_shape=jax.ShapeDtypeStruct((n, D_MODEL), F32),
        grid=(n_steps,),
        in_specs=[
            pl.BlockSpec((tm // PAIRS_PER_IDX_ROW, LANES), tile, memory_space=pltpu.SMEM),
            pl.BlockSpec((tm // PAIRS_PER_IDX_ROW, LANES), lambda i: (jnp.minimum(i + 1, n_steps - 1), 0),
                         memory_space=pltpu.SMEM),
            pl.BlockSpec((tm, D_MODEL), tile),
            pl.BlockSpec((tm, LANES), tile),
            pl.BlockSpec((1, D_MODEL), lambda i: (0, 0)),
            pl.BlockSpec(memory_space=pl.ANY),
        ],
        out_specs=pl.BlockSpec((tm, D_MODEL), tile),
        scratch_shapes=[pltpu.VMEM((2, TOP_K, tm, D_MODEL), F32), pltpu.SemaphoreType.DMA((2,))],
        compiler_params=_cparams(("arbitrary",)),
    )(dest2d, dest2d, x, gates, norm_w.reshape(1, D_MODEL), ys)


def _slot_tables(idx, rank, counts, n_tok):
    tb = MOE_BLOCK
    n_pairs = n_tok * TOP_K
    n_blocks = (n_pairs + N_EXPERTS * (tb - 1) + tb - 1) // tb
    padded = (counts + tb - 1) // tb * tb
    padded_end = jnp.cumsum(padded)
    first = padded_end - padded
    experts = jnp.arange(N_EXPERTS, dtype=jnp.int32)
    dest = jnp.sum(jnp.where(idx[:, :, None] == experts, first, 0), axis=-1) + rank
    dest2d = dest.reshape(n_tok // PAIRS_PER_IDX_ROW, LANES)
    block_start = jnp.arange(n_blocks, dtype=jnp.int32) * tb
    block_expert = jnp.minimum(jnp.sum(padded_end[None, :] <= block_start[:, None], axis=1),
                               N_EXPERTS - 1).astype(jnp.int32)
    n_used = (padded_end[-1] // tb).astype(jnp.int32).reshape(1)
    tab = jnp.stack([first, counts, padded - counts, jnp.broadcast_to(padded_end[-1], (N_EXPERTS,))]).astype(jnp.int32)
    return dest2d, tab, block_expert, n_used, n_blocks * tb


def _pack_in_proj(w_in):
    z, xbc, dtw, qkv, fw, g = jnp.split(
        w_in, [D_MODEL, D_MODEL + CONV_DIM, D_MODEL + CONV_DIM + H_SSD,
               D_MODEL + CONV_DIM + H_SSD + 3 * D_MODEL, D_MODEL + CONV_DIM + H_SSD + 3 * D_MODEL + H_FOX], axis=1)
    g_ssd, g_fox = jnp.split(g, 2, axis=1)
    q_w, kv_w = qkv[:, :D_MODEL] * FOX_QSCALE, qkv[:, D_MODEL:]
    w_main = jnp.concatenate([xbc, z, g_ssd, q_w, kv_w, g_fox], axis=1).astype(BF16)
    pad = jnp.zeros((D_MODEL, LANES - H_FOX - H_SSD), w_in.dtype)
    w_small = jnp.concatenate([fw, dtw, pad], axis=1).astype(BF16)
    return w_main, w_small


def _pad_lanes(f_part, dt_part):
    return jnp.concatenate([f_part, dt_part, jnp.zeros((LANES - H_FOX - H_SSD,), F32)]).reshape(1, LANES)


def kernel(x, norm_mix_w, w_in, conv_w, conv_b, dt_bias, a_log, d_skip, ssd_norm_w, b_forget, w_branch_ssd,
           w_branch_fox, w_out, norm_ffn_w, router_w, router_b, w_gate, b_gate, w_up, b_up, w_down, b_down,
           norm_final_w):
    batch, seq, d = x.shape
    n = batch * seq
    depth = w_in.shape[0]
    x = x.reshape(n, d)
    for l in range(depth):
        w_main, w_small = _pack_in_proj(w_in[l])
        pmain, psmall = norm_in_proj(x, norm_mix_w[l], w_main, w_small)
        sm_bias = _pad_lanes(b_forget[l], dt_bias[l])
        alog_row = _pad_lanes(jnp.zeros((H_FOX,), F32), a_log[l])
        dskip_row = jnp.repeat(d_skip[l], SSD_HEAD_DIM).reshape(1, D_MODEL)
        y_ssd, c = ssd_branch(pmain, psmall, conv_w[l], conv_b[l].reshape(1, CONV_DIM), sm_bias, alog_row,
                              dskip_row, ssd_norm_w[l].reshape(1, D_MODEL), batch, seq)
        o_fox = fox_attention(pmain, c, batch, seq)
        u = branch_merge(y_ssd, o_fox, w_branch_ssd[l].astype(BF16), w_branch_fox[l].astype(BF16), pmain)
        rw_pad = jnp.concatenate([router_w[l], jnp.zeros((D_MODEL, LANES - N_EXPERTS), F32)], axis=1).astype(BF16)
        rb_pad = jnp.concatenate([router_b[l], jnp.full((LANES - N_EXPERTS,), NEG_BIG, F32)]).reshape(1, LANES)
        x, h2, idx_pad, rank_pad, gates_pad, counts_pad = out_proj_route(
            u, w_out[l].astype(BF16), x, norm_ffn_w[l], rw_pad, rb_pad)
        dest2d, tab, block_expert, n_used, n_slots = _slot_tables(
            idx_pad[:, :TOP_K], rank_pad[:, :TOP_K], counts_pad[0, :N_EXPERTS], n)
        xs = moe_dispatch(h2, dest2d, tab, n_slots)
        ys = moe_experts(xs, block_expert, n_used,
                         w_gate[l].astype(BF16), b_gate[l].reshape(N_EXPERTS, 1, D_FF),
                         w_up[l].astype(BF16), b_up[l].reshape(N_EXPERTS, 1, D_FF),
                         w_down[l].astype(BF16), b_down[l].reshape(N_EXPERTS, 1, D_MODEL))
        x = moe_combine(x, ys, dest2d, gates_pad, norm_final_w, final_norm=(l == depth - 1))
    return x.reshape(batch, seq, d)
```

```python
import functools
import math

import jax
import jax.numpy as jnp
from jax import lax
from jax.experimental import pallas as pl
from jax.experimental.pallas import tpu as pltpu

F32 = jnp.float32
BF16 = jnp.bfloat16

D_MODEL = 2048
EPS = 1e-5
SSD_HEAD_DIM = 64
H_SSD = 32
SSD_GROUPS = 8
SSD_STATE = 128
CONV_K = 4
CONV_DIM = D_MODEL + 2 * SSD_GROUPS * SSD_STATE
SSD_CHUNK = 128
FOX_HEAD_DIM = 128
H_FOX = 16
N_EXPERTS = 32
TOP_K = 4
D_FF = 768
SWIGLU_LIMIT = 7.0
SWIGLU_ALPHA = 1.702

LANES = 128
VMEM_LIMIT = 56 * 1024 * 1024
NEG_BIG = -1e30

COL_XBC, COL_Z, COL_GSSD, COL_Q, COL_K, COL_V, COL_GFOX = 0, 2, 3, 4, 5, 6, 7
SM_DT0 = H_FOX

MOE_BLOCK = 512


def _cparams(sem):
    return pltpu.CompilerParams(dimension_semantics=sem, vmem_limit_bytes=VMEM_LIMIT)


def _dot(a, b):
    return jnp.dot(a, b, preferred_element_type=F32)


def _dot_nt(a, b):
    return lax.dot_general(a, b, (((1,), (1,)), ((), ())), preferred_element_type=F32)


def _sigmoid(x):
    return 0.5 * jnp.tanh(0.5 * x) + 0.5


def _norm_proj_kernel(x_ref, nw_ref, w_ref, ws_ref, o_ref, os_ref, h_scr):
    @pl.when(pl.program_id(1) == 0)
    def _():
        rb = min(256, x_ref.shape[0])
        for r in range(x_ref.shape[0] // rb):
            rows = slice(r * rb, (r + 1) * rb)
            x = x_ref[rows, :]
            h = (x * lax.rsqrt(jnp.mean(x * x, axis=-1, keepdims=True) + EPS) * nw_ref[...]).astype(BF16)
            h_scr[rows, :] = h
            os_ref[rows, :] = _dot(h, ws_ref[...])

    o_ref[...] = _dot(h_scr[...], w_ref[...]).astype(o_ref.dtype)


def norm_in_proj(x, norm_w, w_main, w_small, tm=1024, tn=1024):
    n, d = x.shape
    n_main, n_small = w_main.shape[1], w_small.shape[1]
    tm = min(tm, n)
    return pl.pallas_call(
        _norm_proj_kernel,
        out_shape=(jax.ShapeDtypeStruct((n, n_main), BF16), jax.ShapeDtypeStruct((n, n_small), F32)),
        grid=(n // tm, n_main // tn),
        in_specs=[
            pl.BlockSpec((tm, d), lambda i, j: (i, 0)),
            pl.BlockSpec((1, d), lambda i, j: (0, 0)),
            pl.BlockSpec((d, tn), lambda i, j: (0, j)),
            pl.BlockSpec((d, n_small), lambda i, j: (0, 0)),
        ],
        out_specs=(pl.BlockSpec((tm, tn), lambda i, j: (i, j)),
                   pl.BlockSpec((tm, n_small), lambda i, j: (i, 0))),
        scratch_shapes=[pltpu.VMEM((tm, d), BF16)],
        compiler_params=_cparams(("parallel", "arbitrary")),
    )(x, norm_w.reshape(1, d), w_main, w_small)


def _cumsum_rows(val):
    L = val.shape[0]
    row = lax.broadcasted_iota(jnp.int32, (L, L), 0)
    col = lax.broadcasted_iota(jnp.int32, (L, L), 1)
    tri = jnp.where(row >= col, 1.0, 0.0).astype(BF16)
    hi = val.astype(BF16)
    r1 = val - hi.astype(F32)
    mid = r1.astype(BF16)
    lo = (r1 - mid.astype(F32)).astype(BF16)
    return _dot(tri, hi) + _dot(tri, mid) + _dot(tri, lo)


def _ssd_kernel(xbc_ref, z_ref, sm_ref, convw_ref, convb_ref, bias_ref, alog_ref, dskip_ref, normw_ref,
                y_ref, c_ref, ubuf, xc, ybuf, state, carry):
    L = SSD_CHUNK
    ci = pl.program_id(1)

    @pl.when(ci == 0)
    def _():
        ubuf[0:8, :] = jnp.zeros((8, CONV_DIM), F32)
        state[...] = jnp.zeros(state.shape, F32)
        carry[...] = jnp.zeros(carry.shape, F32)

    ubuf[8:8 + L, :] = xbc_ref[...].astype(F32)
    ct_w = 512
    for t in range(CONV_DIM // ct_w):
        sl = slice(t * ct_w, (t + 1) * ct_w)
        u = ubuf[:, sl]
        u1 = pltpu.roll(u, 1, axis=0)
        near = convw_ref[3:4, sl] * u + convw_ref[2:3, sl] * u1
        far = convw_ref[1:2, sl] * u + convw_ref[0:1, sl] * u1
        acc = (near + pltpu.roll(far, 2, axis=0))[8:, :] + convb_ref[:, sl]
        xc[:, sl] = acc * _sigmoid(acc)
    ubuf[0:8, :] = ubuf[L:L + 8, :]

    lane = lax.broadcasted_iota(jnp.int32, (L, LANES), 1)
    lane1 = lax.broadcasted_iota(jnp.int32, (1, LANES), 1)
    v = sm_ref[...] + bias_ref[...]
    sp = jnp.log1p(jnp.exp(-jnp.abs(v)))
    logf = jnp.minimum(v, 0.0) - sp
    dt = jnp.maximum(v, 0.0) + sp
    is_dt1 = (lane1 >= SM_DT0) & (lane1 < SM_DT0 + H_SSD)
    a_row = jnp.where(is_dt1, -jnp.exp(alog_ref[...]), 0.0)
    cs = _cumsum_rows(jnp.where(lane < SM_DT0, logf, dt * a_row))
    c_full = cs + carry[...]
    c_ref[...] = c_full
    carry[...] = c_full[L - 1:L, :]

    cs_t = cs.T
    last = cs[L - 1:L, :]
    e_in = jnp.exp(cs)
    e_out = jnp.exp(last - cs)
    e_all = jnp.exp(last)
    row = lax.broadcasted_iota(jnp.int32, (L, L), 0)
    col = lax.broadcasted_iota(jnp.int32, (L, L), 1)
    causal = row >= col
    lo_half = lane < SSD_HEAD_DIM
    lo_half1 = lane1 < SSD_HEAD_DIM

    for g in range(SSD_GROUPS):
        bm = xc[:, D_MODEL + g * SSD_STATE:D_MODEL + (g + 1) * SSD_STATE]
        cm = xc[:, D_MODEL + (SSD_GROUPS + g) * SSD_STATE:D_MODEL + (SSD_GROUPS + g + 1) * SSD_STATE]
        bmb = bm.astype(BF16)
        cmb = cm.astype(BF16)
        cb = _dot_nt(cmb, bmb)
        bm_t = bm.T.astype(BF16)
        for pr in range(2):
            h0 = g * 4 + pr * 2
            k0, k1 = SM_DT0 + h0, SM_DT0 + h0 + 1
            pidx = h0 // 2

            def pair(qarr, k0=k0, k1=k1):
                return jnp.where(lo_half, qarr[:, k0:k0 + 1], qarr[:, k1:k1 + 1])

            xs_p = xc[:, h0 * SSD_HEAD_DIM:h0 * SSD_HEAD_DIM + LANES]
            xd = xs_p * pair(dt)
            xdb = xd.astype(BF16)
            ys = []
            for k in (k0, k1):
                seg = cs[:, k:k + 1] - cs_t[k:k + 1, :]
                lm = jnp.exp(jnp.where(causal, seg, NEG_BIG))
                ys.append(_dot((cb * lm).astype(BF16), xdb))
            y_diag = jnp.where(lo_half, ys[0], ys[1])
            st_prev = state[pidx]
            y_off = _dot(cmb, st_prev.astype(BF16)) * pair(e_in)
            st_new = _dot(bm_t, (xd * pair(e_out)).astype(BF16))
            e_pair = jnp.where(lo_half1, e_all[:, k0:k0 + 1], e_all[:, k1:k1 + 1])
            state[pidx] = e_pair * st_prev + st_new
            psl = slice(h0 * SSD_HEAD_DIM, h0 * SSD_HEAD_DIM + LANES)
            ybuf[:, psl] = y_diag + y_off + dskip_ref[:, psl] * xs_p

    gw = D_MODEL // SSD_GROUPS
    for g in range(SSD_GROUPS):
        sl = slice(g * gw, (g + 1) * gw)
        zz = z_ref[:, sl].astype(F32)
        gz = ybuf[:, sl] * (zz * _sigmoid(zz))
        gz = gz * lax.rsqrt(jnp.mean(gz * gz, axis=-1, keepdims=True) + EPS)
        y_ref[:, sl] = (gz * normw_ref[:, sl]).astype(y_ref.dtype)


def ssd_branch(pmain, psmall, conv_w, conv_b, sm_bias, alog_row, dskip_row, norm_w, batch, seq):
    n = batch * seq
    L = SSD_CHUNK
    nc = seq // L
    rowmap = lambda b, c: (b * nc + c, 0)
    const = lambda b, c: (0, 0)
    return pl.pallas_call(
        _ssd_kernel,
        out_shape=(jax.ShapeDtypeStruct((n, D_MODEL), BF16),
                   jax.ShapeDtypeStruct((n, LANES), F32)),
        grid=(batch, nc),
        in_specs=[
            pl.BlockSpec((L, CONV_DIM), lambda b, c: (b * nc + c, COL_XBC)),
            pl.BlockSpec((L, D_MODEL), lambda b, c: (b * nc + c, COL_Z)),
            pl.BlockSpec((L, LANES), rowmap),
            pl.BlockSpec((CONV_K, CONV_DIM), const),
            pl.BlockSpec((1, CONV_DIM), const),
            pl.BlockSpec((1, LANES), const),
            pl.BlockSpec((1, LANES), const),
            pl.BlockSpec((1, D_MODEL), const),
            pl.BlockSpec((1, D_MODEL), const),
        ],
        out_specs=(pl.BlockSpec((L, D_MODEL), rowmap),
                   pl.BlockSpec((L, LANES), rowmap)),
        scratch_shapes=[
            pltpu.VMEM((L + 8, CONV_DIM), F32),
            pltpu.VMEM((L, CONV_DIM), F32),
            pltpu.VMEM((L, D_MODEL), F32),
            pltpu.VMEM((H_SSD // 2, SSD_STATE, LANES), F32),
            pltpu.VMEM((1, LANES), F32),
        ],
        compiler_params=_cparams(("parallel", "arbitrary")),
    )(pmain, pmain, psmall, conv_w, conv_b, sm_bias, alog_row, dskip_row, norm_w)


LOG2E = math.log2(math.e)
FOX_AUG = LANES
FOX_QSCALE = LOG2E / math.sqrt(FOX_HEAD_DIM)


def _split3_lanes(vals, first):
    rows = vals.shape[0]
    hi = vals.astype(BF16).astype(F32)
    r1 = vals - hi
    mid = r1.astype(BF16).astype(F32)
    lo = (r1 - mid).astype(BF16).astype(F32)
    lane = lax.broadcasted_iota(jnp.int32, (rows, FOX_AUG), 1)
    ones = jnp.where(lane < 6, 1.0, 0.0)
    return jnp.where(lane == first, hi, jnp.where(lane == first + 1, mid, jnp.where(lane == first + 2, lo, ones)))


def _head_column(c_tile, h):
    lane = lax.broadcasted_iota(jnp.int32, c_tile.shape, 1)
    return jnp.sum(jnp.where(lane == h, c_tile, 0.0), axis=1, keepdims=True)


def _fox_kernel(q_ref, k_ref, v_ref, cq_ref, call_ref, o_ref, kaug, *, tq, seq, nh):
    hp = pl.program_id(1)
    qi = pl.program_id(2)
    rb = min(256, seq)
    hd = FOX_HEAD_DIM

    @pl.when(qi == 0)
    def _():
        for hh in range(nh):
            for r in range(seq // rb):
                rows = slice(r * rb, (r + 1) * rb)
                ck = _head_column(call_ref[rows, :], hp * nh + hh) * LOG2E
                kaug[hh, rows, 0:hd] = k_ref[rows, hh * hd:(hh + 1) * hd]
                kaug[hh, rows, hd:hd + FOX_AUG] = _split3_lanes(-ck, 3).astype(BF16)

    q_aug = []
    for hh in range(nh):
        cq = _head_column(cq_ref[...], hp * nh + hh) * LOG2E
        q_aug.append(jnp.concatenate([q_ref[:, hh * hd:(hh + 1) * hd], _split3_lanes(cq, 0).astype(BF16)], axis=1))
    row = lax.broadcasted_iota(jnp.int32, (tq, tq), 0)
    col = lax.broadcasted_iota(jnp.int32, (tq, tq), 1)

    def step(j, carry, masked):
        off = pl.multiple_of(j * tq, tq)
        heads = range(nh)
        ss = [_dot_nt(q_aug[hh], kaug[hh, pl.ds(off, tq), :]) for hh in heads]
        if masked:
            ss = [jnp.where(row >= col, s, NEG_BIG) for s in ss]
        ms = [jnp.maximum(carry[hh][0], jnp.max(ss[hh], axis=1, keepdims=True)) for hh in heads]
        ps = [jnp.exp2(ss[hh] - ms[hh]) for hh in heads]
        alphas = [jnp.exp2(carry[hh][0] - ms[hh]) for hh in heads]
        ls = [alphas[hh] * carry[hh][1] + jnp.sum(ps[hh], axis=1, keepdims=True) for hh in heads]
        accs = [alphas[hh] * carry[hh][2] + _dot(ps[hh].astype(BF16), v_ref[pl.ds(off, tq), hh * hd:(hh + 1) * hd])
                for hh in heads]
        return tuple((ms[hh], ls[hh], accs[hh]) for hh in heads)

    init = tuple((jnp.full((tq, 1), NEG_BIG, F32), jnp.zeros((tq, 1), F32), jnp.zeros((tq, hd), F32))
                 for _ in range(nh))
    carry = lax.fori_loop(0, qi, lambda j, c: step(j, c, False), init)
    carry = step(qi, carry, True)
    for hh in range(nh):
        _, l, acc = carry[hh]
        o_ref[:, hh * hd:(hh + 1) * hd] = (acc / l).astype(o_ref.dtype)


FOX_HEADS_PER_STEP = 4


def fox_attention(pmain, c, batch, seq, tq=512):
    n = batch * seq
    tq = min(tq, seq)
    nq = seq // tq
    nh = FOX_HEADS_PER_STEP
    wb = nh * FOX_HEAD_DIM
    per_d = D_MODEL // wb
    return pl.pallas_call(
        functools.partial(_fox_kernel, tq=tq, seq=seq, nh=nh),
        out_shape=jax.ShapeDtypeStruct((n, D_MODEL), BF16),
        grid=(batch, H_FOX // nh, nq),
        in_specs=[
            pl.BlockSpec((tq, wb), lambda b, h, i: (b * nq + i, COL_Q * per_d + h)),
            pl.BlockSpec((seq, wb), lambda b, h, i: (b, COL_K * per_d + h)),
            pl.BlockSpec((seq, wb), lambda b, h, i: (b, COL_V * per_d + h)),
            pl.BlockSpec((tq, LANES), lambda b, h, i: (b * nq + i, 0)),
            pl.BlockSpec((seq, LANES), lambda b, h, i: (b, 0)),
        ],
        out_specs=pl.BlockSpec((tq, wb), lambda b, h, i: (b * nq + i, h)),
        scratch_shapes=[pltpu.VMEM((nh, seq, FOX_HEAD_DIM + FOX_AUG), BF16)],
        compiler_params=_cparams(("parallel", "parallel", "arbitrary")),
    )(pmain, pmain, pmain, c, c)


def _merge_kernel(y_ref, o_ref, wbs_ref, wbf_ref, gs_ref, gf_ref, u_ref):
    a = _dot(y_ref[...], wbs_ref[...])
    b = _dot(o_ref[...], wbf_ref[...])
    u = _sigmoid(gs_ref[...].astype(F32)) * a + _sigmoid(gf_ref[...].astype(F32)) * b
    u_ref[...] = u.astype(u_ref.dtype)


def branch_merge(y_ssd, o_fox, wbs, wbf, pmain, tm=1024, tn=512):
    n = y_ssd.shape[0]
    tm = min(tm, n)
    per_d = D_MODEL // tn
    return pl.pallas_call(
        _merge_kernel,
        out_shape=jax.ShapeDtypeStruct((n, D_MODEL), BF16),
        grid=(n // tm, D_MODEL // tn),
        in_specs=[
            pl.BlockSpec((tm, D_MODEL), lambda i, j: (i, 0)),
            pl.BlockSpec((tm, D_MODEL), lambda i, j: (i, 0)),
            pl.BlockSpec((D_MODEL, tn), lambda i, j: (0, j)),
            pl.BlockSpec((D_MODEL, tn), lambda i, j: (0, j)),
            pl.BlockSpec((tm, tn), lambda i, j: (i, COL_GSSD * per_d + j)),
            pl.BlockSpec((tm, tn), lambda i, j: (i, COL_GFOX * per_d + j)),
        ],
        out_specs=pl.BlockSpec((tm, tn), lambda i, j: (i, j)),
        compiler_params=_cparams(("parallel", "arbitrary")),
    )(y_ssd, o_fox, wbs, wbf, pmain, pmain)


def _outproj_kernel(u_ref, w_ref, x_ref, o_ref):
    o_ref[...] = x_ref[...] + _dot(u_ref[...], w_ref[...])


def out_proj_residual(u, w_out, x, tm=1024, tn=1024):
    n = u.shape[0]
    tm = min(tm, n)
    return pl.pallas_call(
        _outproj_kernel,
        out_shape=jax.ShapeDtypeStruct((n, D_MODEL), F32),
        grid=(n // tm, D_MODEL // tn),
        in_specs=[
            pl.BlockSpec((tm, D_MODEL), lambda i, j: (i, 0)),
            pl.BlockSpec((D_MODEL, tn), lambda i, j: (0, j)),
            pl.BlockSpec((tm, tn), lambda i, j: (i, j)),
        ],
        out_specs=pl.BlockSpec((tm, tn), lambda i, j: (i, j)),
        compiler_params=_cparams(("parallel", "arbitrary")),
    )(u, w_out, x)


def _router_kernel(x_ref, nw_ref, rw_ref, rb_ref, h_ref, idx_ref, rank_ref, gate_ref, cnt_ref, carry):
    tm = x_ref.shape[0]

    @pl.when(pl.program_id(0) == 0)
    def _():
        carry[...] = jnp.zeros(carry.shape, F32)

    x = x_ref[...]
    h = x * lax.rsqrt(jnp.mean(x * x, axis=-1, keepdims=True) + EPS) * nw_ref[...]
    h_ref[...] = h
    work = _dot(h.astype(BF16), rw_ref[...]) + rb_ref[...]
    lane = lax.broadcasted_iota(jnp.int32, work.shape, 1)
    tops, idxs = [], []
    for _ in range(TOP_K):
        mx = jnp.max(work, axis=1, keepdims=True)
        ix = jnp.min(jnp.where(work == mx, lane, LANES), axis=1, keepdims=True)
        tops.append(mx)
        idxs.append(ix)
        work = jnp.where(lane == ix, -jnp.inf, work)
    es = [jnp.exp(t - tops[0]) for t in tops]
    den = es[0] + es[1] + es[2] + es[3]

    chosen = jnp.zeros(work.shape, F32)
    for k in range(TOP_K):
        chosen = jnp.where(lane == idxs[k], 1.0, chosen)
    row = lax.broadcasted_iota(jnp.int32, (tm, tm), 0)
    col = lax.broadcasted_iota(jnp.int32, (tm, tm), 1)
    before = _dot(jnp.where(col < row, 1.0, 0.0).astype(BF16), chosen.astype(BF16)) + carry[...]
    total = before[tm - 1:tm, :] + chosen[tm - 1:tm, :]
    carry[...] = total
    cnt_ref[...] = total.astype(jnp.int32)

    gates = jnp.zeros(work.shape, F32)
    idx = jnp.zeros(work.shape, jnp.int32)
    rank = jnp.zeros(work.shape, jnp.int32)
    for k in range(TOP_K):
        rk = jnp.sum(jnp.where(lane == idxs[k], before, 0.0), axis=1, keepdims=True).astype(jnp.int32)
        gates = jnp.where(lane == k, es[k] / den, gates)
        idx = jnp.where(lane == k, idxs[k], idx)
        rank = jnp.where(lane == k, rk, rank)
    gate_ref[...] = gates
    idx_ref[...] = idx
    rank_ref[...] = rank


def ffn_norm_router(x, norm_w, rw_pad, rb_pad, tm=512):
    n = x.shape[0]
    tm = min(tm, n)
    tile = lambda i: (i, 0)
    const = lambda i: (0, 0)
    return pl.pallas_call(
        _router_kernel,
        out_shape=(jax.ShapeDtypeStruct((n, D_MODEL), F32),
                   jax.ShapeDtypeStruct((n, LANES), jnp.int32),
                   jax.ShapeDtypeStruct((n, LANES), jnp.int32),
                   jax.ShapeDtypeStruct((n, LANES), F32),
                   jax.ShapeDtypeStruct((1, LANES), jnp.int32)),
        grid=(n // tm,),
        in_specs=[
            pl.BlockSpec((tm, D_MODEL), tile),
            pl.BlockSpec((1, D_MODEL), const),
            pl.BlockSpec((D_MODEL, LANES), const),
            pl.BlockSpec((1, LANES), const),
        ],
        out_specs=(pl.BlockSpec((tm, D_MODEL), tile),
                   pl.BlockSpec((tm, LANES), tile),
                   pl.BlockSpec((tm, LANES), tile),
                   pl.BlockSpec((tm, LANES), tile),
                   pl.BlockSpec((1, LANES), const)),
        scratch_shapes=[pltpu.VMEM((1, LANES), F32)],
        compiler_params=_cparams(("arbitrary",)),
    )(x, norm_w.reshape(1, D_MODEL), rw_pad, rb_pad)


PAIRS_PER_IDX_ROW = LANES // TOP_K


def _dispatch_kernel(tab_ref, dest_ref, h_ref, xs_hbm, zrow, sem, zsem, *, tm):
    i = pl.program_id(0)

    def issue(rr, c):
        for cc in range(LANES):
            r = rr * PAIRS_PER_IDX_ROW + cc // TOP_K
            pltpu.make_async_copy(h_ref.at[pl.ds(r, 1)], xs_hbm.at[pl.ds(dest_ref[rr, cc], 1)], sem).start(
                priority=cc % 2)
        return c

    lax.fori_loop(0, tm // PAIRS_PER_IDX_ROW, issue, 0)

    @pl.when(i == 0)
    def _():
        zrow[...] = jnp.zeros(zrow.shape, F32)

        def per_expert(e, c):
            first = tab_ref[0, e] + tab_ref[1, e]
            n_pad = tab_ref[2, e]

            def zero_row(r, c2):
                pltpu.make_async_copy(zrow.at[pl.ds(0, 1)], xs_hbm.at[pl.ds(first + r, 1)], zsem).start()
                return c2

            def wait_row(r, c2):
                pltpu.make_async_copy(zrow.at[pl.ds(0, 1)], xs_hbm.at[pl.ds(first + r, 1)], zsem).wait()
                return c2

            lax.fori_loop(0, n_pad, zero_row, 0)
            lax.fori_loop(0, n_pad, wait_row, 0)
            return c

        lax.fori_loop(0, N_EXPERTS, per_expert, 0)

        used = tab_ref[3, 0]
        n_tail = lax.shift_right_logical(xs_hbm.shape[0] - used, 3)

        def tail_copy(j):
            return pltpu.make_async_copy(zrow, xs_hbm.at[pl.ds(pl.multiple_of(used + j * 8, 8), 8)], zsem)

        lax.fori_loop(0, n_tail, lambda j, c: (tail_copy(j).start(), c)[1], 0)
        lax.fori_loop(0, n_tail, lambda j, c: (tail_copy(j).wait(), c)[1], 0)

    pltpu.make_async_copy(xs_hbm.at[pl.ds(0, TOP_K * tm)], xs_hbm.at[pl.ds(0, TOP_K * tm)], sem).wait()


def moe_dispatch(h2, dest2d, tab, n_slots, tm=512):
    n = h2.shape[0]
    tm = min(tm, n)
    return pl.pallas_call(
        functools.partial(_dispatch_kernel, tm=tm),
        out_shape=jax.ShapeDtypeStruct((n_slots, D_MODEL), F32),
        grid_spec=pltpu.PrefetchScalarGridSpec(
            num_scalar_prefetch=1,
            grid=(n // tm,),
            in_specs=[
                pl.BlockSpec((tm // PAIRS_PER_IDX_ROW, LANES), lambda i, tab: (i, 0), memory_space=pltpu.SMEM),
                pl.BlockSpec((tm, D_MODEL), lambda i, tab: (i, 0)),
            ],
            out_specs=pl.BlockSpec(memory_space=pl.ANY),
            scratch_shapes=[pltpu.VMEM((8, D_MODEL), F32), pltpu.SemaphoreType.DMA(()), pltpu.SemaphoreType.DMA(())],
        ),
        compiler_params=_cparams(("arbitrary",)),
    )(tab, dest2d, h2)


def _moe_kernel(be_ref, nu_ref, xs_ref, wg_ref, bg_ref, wu_ref, bu_ref, wd_ref, bd_ref, y_ref):
    i = pl.program_id(0)

    @pl.when(i < nu_ref[0])
    def _():
        xb = xs_ref[...].astype(BF16)
        a = jnp.minimum(_dot(xb, wg_ref[0]) + bg_ref[0], SWIGLU_LIMIT)
        up = jnp.clip(_dot(xb, wu_ref[0]) + bu_ref[0], -SWIGLU_LIMIT, SWIGLU_LIMIT)
        act = (up + 1.0) * (a * _sigmoid(SWIGLU_ALPHA * a))
        y_ref[...] = _dot(act.astype(BF16), wd_ref[0]) + bd_ref[0]

    @pl.when(i >= nu_ref[0])
    def _():
        y_ref[...] = jnp.zeros(y_ref.shape, y_ref.dtype)


def moe_experts(xs, block_expert, n_used, wg, bg, wu, bu, wd, bd):
    n_slots = xs.shape[0]
    tb = MOE_BLOCK
    emap = lambda i, be, nu: (be[i], 0, 0)
    xmap = lambda i, be, nu: (jnp.minimum(i, nu[0] - 1), 0)
    return pl.pallas_call(
        _moe_kernel,
        out_shape=jax.ShapeDtypeStruct((n_slots, D_MODEL), F32),
        grid_spec=pltpu.PrefetchScalarGridSpec(
            num_scalar_prefetch=2,
            grid=(n_slots // tb,),
            in_specs=[
                pl.BlockSpec((tb, D_MODEL), xmap),
                pl.BlockSpec((1, D_MODEL, D_FF), emap),
                pl.BlockSpec((1, 1, D_FF), emap),
                pl.BlockSpec((1, D_MODEL, D_FF), emap),
                pl.BlockSpec((1, 1, D_FF), emap),
                pl.BlockSpec((1, D_FF, D_MODEL), emap),
                pl.BlockSpec((1, 1, D_MODEL), emap),
            ],
            out_specs=pl.BlockSpec((tb, D_MODEL), lambda i, be, nu: (i, 0)),
        ),
        compiler_params=_cparams(("arbitrary",)),
    )(block_expert, n_used, xs, wg, bg, wu, bu, wd, bd)


def _combine_kernel(dcur_ref, dnext_ref, x_ref, g_ref, nw_ref, ys_hbm, o_ref, buf, sems, *, tm, final_norm):
    i = pl.program_id(0)
    n_steps = pl.num_programs(0)
    slot = i % 2

    def issue(dest_ref, s):
        def body(rr, c):
            for cc in range(LANES):
                r = rr * PAIRS_PER_IDX_ROW + cc // TOP_K
                pltpu.make_async_copy(ys_hbm.at[pl.ds(dest_ref[rr, cc], 1)],
                                      buf.at[s, cc % TOP_K, pl.ds(r, 1)], sems.at[s]).start(priority=cc % 2)
            return c
        lax.fori_loop(0, tm // PAIRS_PER_IDX_ROW, body, 0)

    @pl.when(i == 0)
    def _():
        issue(dcur_ref, 0)

    @pl.when(i + 1 < n_steps)
    def _():
        issue(dnext_ref, 1 - slot)

    for k in range(TOP_K):
        pltpu.make_async_copy(ys_hbm.at[pl.ds(0, tm)], buf.at[slot, k], sems.at[slot]).wait()
    g = g_ref[...]
    acc = x_ref[...]
    for k in range(TOP_K):
        acc = acc + g[:, k:k + 1] * buf[slot, k]
    if final_norm:
        acc = acc * lax.rsqrt(jnp.mean(acc * acc, axis=-1, keepdims=True) + EPS) * nw_ref[...]
    o_ref[...] = acc


def moe_combine(x, ys, dest2d, gates, norm_w, final_norm, tm=256):
    n = x.shape[0]
    tm = min(tm, n)
    n_steps = n // tm
    tile = lambda i: (i, 0)
    return pl.pallas_call(
        functools.partial(_combine_kernel, tm=tm, final_norm=final_norm),
        out_shape=jax.ShapeDtypeStruct((n, D_MODEL), F32),
        grid=(n_steps,),
        in_specs=[
            pl.BlockSpec((tm // PAIRS_PER_IDX_ROW, LANES), tile, memory_space=pltpu.SMEM),
            pl.BlockSpec((tm // PAIRS_PER_IDX_ROW, LANES), lambda i: (jnp.minimum(i + 1, n_steps - 1), 0),
                         memory_space=pltpu.SMEM),
            pl.BlockSpec((tm, D_MODEL), tile),
            pl.BlockSpec((tm, LANES), tile),
            pl.BlockSpec((1, D_MODEL), lambda i: (0, 0)),
            pl.BlockSpec(memory_space=pl.ANY),
        ],
        out_specs=pl.BlockSpec((tm, D_MODEL), tile),
        scratch_shapes=[pltpu.VMEM((2, TOP_K, tm, D_MODEL), F32), pltpu.SemaphoreType.DMA((2,))],
        compiler_params=_cparams(("arbitrary",)),
    )(dest2d, dest2d, x, gates, norm_w.reshape(1, D_MODEL), ys)


def _slot_tables(idx, rank, counts, n_tok):
    tb = MOE_BLOCK
    n_pairs = n_tok * TOP_K
    n_blocks = (n_pairs + N_EXPERTS * (tb - 1) + tb - 1) // tb
    padded = (counts + tb - 1) // tb * tb
    padded_end = jnp.cumsum(padded)
    first = padded_end - padded
    experts = jnp.arange(N_EXPERTS, dtype=jnp.int32)
    dest = jnp.sum(jnp.where(idx[:, :, None] == experts, first, 0), axis=-1) + rank
    dest2d = dest.reshape(n_tok // PAIRS_PER_IDX_ROW, LANES)
    block_start = jnp.arange(n_blocks, dtype=jnp.int32) * tb
    block_expert = jnp.minimum(jnp.sum(padded_end[None, :] <= block_start[:, None], axis=1),
                               N_EXPERTS - 1).astype(jnp.int32)
    n_used = (padded_end[-1] // tb).astype(jnp.int32).reshape(1)
    tab = jnp.stack([first, counts, padded - counts, jnp.broadcast_to(padded_end[-1], (N_EXPERTS,))]).astype(jnp.int32)
    return dest2d, tab, block_expert, n_used, n_blocks * tb


def _pack_in_proj(w_in):
    z, xbc, dtw, qkv, fw, g = jnp.split(
        w_in, [D_MODEL, D_MODEL + CONV_DIM, D_MODEL + CONV_DIM + H_SSD,
               D_MODEL + CONV_DIM + H_SSD + 3 * D_MODEL, D_MODEL + CONV_DIM + H_SSD + 3 * D_MODEL + H_FOX], axis=1)
    g_ssd, g_fox = jnp.split(g, 2, axis=1)
    q_w, kv_w = qkv[:, :D_MODEL] * FOX_QSCALE, qkv[:, D_MODEL:]
    w_main = jnp.concatenate([xbc, z, g_ssd, q_w, kv_w, g_fox], axis=1).astype(BF16)
    pad = jnp.zeros((D_MODEL, LANES - H_FOX - H_SSD), w_in.dtype)
    w_small = jnp.concatenate([fw, dtw, pad], axis=1).astype(BF16)
    return w_main, w_small


def _pad_lanes(f_part, dt_part):
    return jnp.concatenate([f_part, dt_part, jnp.zeros((LANES - H_FOX - H_SSD,), F32)]).reshape(1, LANES)


def kernel(x, norm_mix_w, w_in, conv_w, conv_b, dt_bias, a_log, d_skip, ssd_norm_w, b_forget, w_branch_ssd,
           w_branch_fox, w_out, norm_ffn_w, router_w, router_b, w_gate, b_gate, w_up, b_up, w_down, b_down,
           norm_final_w):
    batch, seq, d = x.shape
    n = batch * seq
    depth = w_in.shape[0]
    x = x.reshape(n, d)
    for l in range(depth):
        w_main, w_small = _pack_in_proj(w_in[l])
        pmain, psmall = norm_in_proj(x, norm_mix_w[l], w_main, w_small)
        sm_bias = _pad_lanes(b_forget[l], dt_bias[l])
        alog_row = _pad_lanes(jnp.zeros((H_FOX,), F32), a_log[l])
        dskip_row = jnp.repeat(d_skip[l], SSD_HEAD_DIM).reshape(1, D_MODEL)
        y_ssd, c = ssd_branch(pmain, psmall, conv_w[l], conv_b[l].reshape(1, CONV_DIM), sm_bias, alog_row,
                              dskip_row, ssd_norm_w[l].reshape(1, D_MODEL), batch, seq)
        o_fox = fox_attention(pmain, c, batch, seq)
        u = branch_merge(y_ssd, o_fox, w_branch_ssd[l].astype(BF16), w_branch_fox[l].astype(BF16), pmain)
        x = out_proj_residual(u, w_out[l].astype(BF16), x)
        rw_pad = jnp.concatenate([router_w[l], jnp.zeros((D_MODEL, LANES - N_EXPERTS), F32)], axis=1).astype(BF16)
        rb_pad = jnp.concatenate([router_b[l], jnp.full((LANES - N_EXPERTS,), NEG_BIG, F32)]).reshape(1, LANES)
        h2, idx_pad, rank_pad, gates_pad, counts_pad = ffn_norm_router(x, norm_ffn_w[l], rw_pad, rb_pad)
        dest2d, tab, block_expert, n_used, n_slots = _slot_tables(
            idx_pad[:, :TOP_K], rank_pad[:, :TOP_K], counts_pad[0, :N_EXPERTS], n)
        xs = moe_dispatch(h2, dest2d, tab, n_slots)
        ys = moe_experts(xs, block_expert, n_used,
                         w_gate[l].astype(BF16), b_gate[l].reshape(N_EXPERTS, 1, D_FF),
                         w_up[l].astype(BF16), b_up[l].reshape(N_EXPERTS, 1, D_FF),
                         w_down[l].astype(BF16), b_down[l].reshape(N_EXPERTS, 1, D_MODEL))
        x = moe_combine(x, ys, dest2d, gates_pad, norm_final_w, final_norm=(l == depth - 1))
    return x.reshape(batch, seq, d)
```
